```python
import math
import jax, jax.numpy as jnp
from jax import lax
import numpy as np

D_MODEL = 1024
BATCH = 8
SEQ = 2048
DEPTH = 4


f32 = jnp.float32

GRID_W = 64
CTX_LEN = 256
D_MIX = D_MODEL
POOL_WIDTH = D_MIX // 4
POOL_WINDOWS = (2, 4, 8, 16)
POOL_GROUP = POOL_WIDTH // len(POOL_WINDOWS)
SSM_WIDTH = D_MIX // 4
SSM_GROUP_CH = 16
SSM_GROUPS = SSM_WIDTH // SSM_GROUP_CH
SSM_STATE = 64
ATTN_WIDTH = D_MIX - POOL_WIDTH - SSM_WIDTH
HEAD_DIM = 64
N_HEADS = ATTN_WIDTH // HEAD_DIM
N_KV_HEADS = 2
GROUP = N_HEADS // N_KV_HEADS
KV_WIDTH = N_KV_HEADS * HEAD_DIM
WINDOW = 128
BLOCK = 128
ROPE_BASE = 10000.0
ROPE_FREQS = HEAD_DIM // 4
IN_SPLITS = (POOL_WIDTH, POOL_WIDTH + SSM_WIDTH, POOL_WIDTH + SSM_WIDTH + ATTN_WIDTH,
             POOL_WIDTH + SSM_WIDTH + ATTN_WIDTH + KV_WIDTH)
IN_WIDTH = POOL_WIDTH + SSM_WIDTH + ATTN_WIDTH + 2 * KV_WIDTH
D_FF = -(-8 * D_MODEL // (3 * 256)) * 256
N_MOD = 6
EPS = 1e-6

kernel_name = "hybrid_pool_s5_swa_prefix_dit"


def _rms(x, g):
    x32 = x.astype(f32)
    y = x32 * lax.rsqrt(jnp.mean(x32 * x32, axis=-1, keepdims=True) + EPS)
    return y * g.astype(f32)


def _rmsnorm(x, g):
    return _rms(x, g).astype(x.dtype)


def _axial_rope_tables(L):
    rows = L // GRID_W
    pos = jnp.arange(L)
    row = jnp.repeat(jnp.arange(rows), GRID_W, total_repeat_length=L).astype(f32)
    col = (pos % GRID_W).astype(f32)
    inv = jnp.power(ROPE_BASE, -jnp.arange(ROPE_FREQS, dtype=f32) / ROPE_FREQS)
    ang = jnp.stack([row[:, None] * inv, col[:, None] * inv], axis=1)
    return jnp.cos(ang), jnp.sin(ang)


def _apply_axial_rope(x, cos, sin):
    xs = x.reshape(x.shape[:-1] + (2, 2, ROPE_FREQS))
    x1, x2 = xs[..., 0, :], xs[..., 1, :]
    c = cos[None, :, None]
    s = sin[None, :, None]
    return jnp.stack([x1 * c - x2 * s, x2 * c + x1 * s], axis=-2).reshape(x.shape)


def _pool_mixer(u, pool_w, pool_scale):
    B, L, _ = u.shape
    u32 = u.astype(f32)
    cs = jnp.pad(jnp.cumsum(u32, axis=1), ((0, 0), (1, 0), (0, 0)))
    us = u32.reshape(B, L, len(POOL_WINDOWS), POOL_GROUP)
    css = cs.reshape(B, L + 1, len(POOL_WINDOWS), POOL_GROUP)
    t = jnp.arange(L)
    diffs = []
    for gi, w in enumerate(POOL_WINDOWS):
        lo = jnp.clip(t - w // 2, 0, L)
        hi = jnp.clip(t + w // 2, 0, L)
        csg = css[:, :, gi]
        mean = (csg[:, hi] - csg[:, lo]) / (hi - lo).astype(f32)[None, :, None]
        diffs.append(mean - us[:, :, gi])
    d = jnp.stack(diffs, axis=2)
    y = jnp.einsum('blgc,gcd->blgd', d, pool_w.astype(f32)).reshape(B, L, POOL_WIDTH)
    return (y * pool_scale.astype(f32)).astype(u.dtype)


def _scan_op(left, right):
    a_l, b_l = left
    a_r, b_r = right
    return a_l * a_r, a_r * b_l + b_r


def _linear_scan(a_bar, bu, s0):
    if s0 is not None:
        bu = bu.at[:, 0].add(a_bar * s0)
    a_full = jnp.broadcast_to(a_bar, bu.shape)
    _, s = lax.associative_scan(_scan_op, (a_full, bu), axis=1)
    return s


def _ssm_discretize(a_re, a_im, log_dt, b_re, b_im):
    lam = lax.complex(jnp.minimum(a_re.astype(f32), -1e-4), a_im.astype(f32))
    dt = jnp.exp(log_dt.astype(f32))[:, None]
    a_bar = jnp.exp(lam * dt)
    b = lax.complex(b_re.astype(f32), b_im.astype(f32))
    b_bar = ((a_bar - 1.0) / lam)[..., None] * b
    return a_bar, b_bar


def _ssm_mixer(u_x, u_c, a_re, a_im, log_dt, b_re, b_im, c_re, c_im, d_skip, glu_w, glu_b, need_ctx):
    B, L, _ = u_x.shape
    Lc = u_c.shape[1]
    ux = u_x.astype(f32).reshape(B, L, SSM_GROUPS, SSM_GROUP_CH)
    uc = u_c.astype(f32).reshape(B, Lc, SSM_GROUPS, SSM_GROUP_CH)
    dg = d_skip.astype(f32).reshape(SSM_GROUPS, SSM_GROUP_CH)
    y_x = dg * ux
    y_c = dg * uc if need_ctx else None
    for direction in (0, 1):
        a_bar, b_bar = _ssm_discretize(a_re[direction], a_im[direction], log_dt[direction],
                                       b_re[direction], b_im[direction])
        cmat = lax.complex(c_re[direction].astype(f32), c_im[direction].astype(f32))
        bu_c = jnp.einsum('gpc,blgc->blgp', b_bar, uc)
        bu_x = jnp.einsum('gpc,blgc->blgp', b_bar, ux)
        if direction == 1:
            bu_c = jnp.flip(bu_c, axis=1)
            bu_x = jnp.flip(bu_x, axis=1)
        s_c = _linear_scan(a_bar, bu_c, None)
        s_x = _linear_scan(a_bar, bu_x, s_c[:, -1])
        if direction == 1:
            s_c = jnp.flip(s_c, axis=1)
            s_x = jnp.flip(s_x, axis=1)
        y_x = y_x + jnp.real(jnp.einsum('gcp,blgp->blgc', cmat, s_x))
        if need_ctx:
            y_c = y_c + jnp.real(jnp.einsum('gcp,blgp->blgc', cmat, s_c))

    def glu(y):
        y = jax.nn.gelu(y.reshape(y.shape[0], y.shape[1], SSM_WIDTH))
        return y * jax.nn.sigmoid(y @ glu_w.astype(f32) + glu_b.astype(f32))

    out_x = glu(y_x).astype(u_x.dtype)
    out_c = glu(y_c).astype(u_c.dtype) if need_ctx else None
    return out_x, out_c


def _attn_mixer(q_x, k_x, v_x, q_c, k_c, v_c, q_norm, k_norm, sink, cos, sin, need_ctx):
    B, L, _ = q_x.shape
    Lc = k_c.shape[1]
    nb = L // BLOCK
    scale = HEAD_DIM ** -0.5
    q = _apply_axial_rope(_rms(q_x.reshape(B, L, N_HEADS, HEAD_DIM), q_norm), cos, sin)
    k = _apply_axial_rope(_rms(k_x.reshape(B, L, N_KV_HEADS, HEAD_DIM), k_norm), cos, sin)
    v = v_x.reshape(B, L, N_KV_HEADS, HEAD_DIM)
    kc = _rms(k_c.reshape(B, Lc, N_KV_HEADS, HEAD_DIM), k_norm)
    vc = v_c.reshape(B, Lc, N_KV_HEADS, HEAD_DIM)
    sink_g = sink.astype(f32).reshape(N_KV_HEADS, GROUP)

    qb = q.reshape(B, nb, BLOCK, N_KV_HEADS, GROUP, HEAD_DIM)

    def band(t):
        tp = jnp.pad(t, ((0, 0), (BLOCK, BLOCK), (0, 0), (0, 0)))
        tp = tp.reshape(B, nb + 2, BLOCK, N_KV_HEADS, HEAD_DIM)
        return jnp.concatenate([tp[:, :-2], tp[:, 1:-1], tp[:, 2:]], axis=2)

    kb, vb = band(k), band(v)
    s_win = jnp.einsum('bnqhgd,bnkhd->bnhgqk', qb, kb) * scale
    blk = jnp.arange(nb)[:, None, None]
    q_pos = blk * BLOCK + jnp.arange(BLOCK)[None, :, None]
    k_pos = (blk - 1) * BLOCK + jnp.arange(3 * BLOCK)[None, None, :]
    valid = (jnp.abs(k_pos - q_pos) <= WINDOW) & (k_pos >= 0) & (k_pos < L)
    s_win = jnp.where(valid[None, :, None, None], s_win, -jnp.inf)
    s_ctx = jnp.einsum('bnqhgd,bchd->bnhgqc', qb, kc) * scale
    s_sink = jnp.broadcast_to(sink_g[None, None, :, :, None, None], s_win.shape[:-1] + (1,))
    p = jax.nn.softmax(jnp.concatenate([s_win, s_ctx, s_sink], axis=-1), axis=-1)
    o = (jnp.einsum('bnhgqk,bnkhd->bnqhgd', p[..., :3 * BLOCK], vb.astype(f32))
         + jnp.einsum('bnhgqc,bchd->bnqhgd', p[..., 3 * BLOCK:3 * BLOCK + Lc], vc.astype(f32)))
    out_x = o.reshape(B, L, ATTN_WIDTH).astype(q_x.dtype)

    out_c = None
    if need_ctx:
        qc = _rms(q_c.reshape(B, Lc, N_HEADS, HEAD_DIM), q_norm).reshape(B, Lc, N_KV_HEADS, GROUP, HEAD_DIM)
        sc = jnp.einsum('bqhgd,bkhd->bhgqk', qc, kc) * scale
        sc_sink = jnp.broadcast_to(sink_g[None, :, :, None, None], sc.shape[:-1] + (1,))
        pc = jax.nn.softmax(jnp.concatenate([sc, sc_sink], axis=-1), axis=-1)
        oc = jnp.einsum('bhgqk,bkhd->bqhgd', pc[..., :Lc], vc.astype(f32))
        out_c = oc.reshape(B, Lc, ATTN_WIDTH).astype(q_c.dtype)
    return out_x, out_c


def _mixing(hx, hc, w_in, w_out, pool_w, pool_scale, a_re, a_im, log_dt, b_re, b_im, c_re, c_im,
            d_skip, glu_w, glu_b, q_norm, k_norm, sink, cos, sin, need_ctx):
    pool_x, ssm_x, q_x, k_x, v_x = jnp.split(hx @ w_in, IN_SPLITS, axis=-1)
    pool_c, ssm_c, q_c, k_c, v_c = jnp.split(hc @ w_in, IN_SPLITS, axis=-1)
    po_x = _pool_mixer(pool_x, pool_w, pool_scale)
    so_x, so_c = _ssm_mixer(ssm_x, ssm_c, a_re, a_im, log_dt, b_re, b_im, c_re, c_im,
                            d_skip, glu_w, glu_b, need_ctx)
    ao_x, ao_c = _attn_mixer(q_x, k_x, v_x, q_c, k_c, v_c, q_norm, k_norm, sink, cos, sin, need_ctx)
    out_x = jnp.concatenate([po_x, so_x, ao_x], axis=-1) @ w_out
    out_c = None
    if need_ctx:
        po_c = _pool_mixer(pool_c, pool_w, pool_scale)
        out_c = jnp.concatenate([po_c, so_c, ao_c], axis=-1) @ w_out
    return out_x, out_c


def _swiglu(h, w_gate, w_up, w_down):
    return (jax.nn.silu(h @ w_gate) * (h @ w_up)) @ w_down


def setup_inputs(seed: int = 0) -> dict:
    key = jax.random.key(seed)
    ks = jax.random.split(key, 32)
    nrm = jax.random.normal
    D = D_MODEL
    a_im0 = math.pi * jnp.arange(SSM_STATE, dtype=f32)
    return {
        "x": nrm(ks[0], (BATCH, SEQ, D), f32),
        "c": nrm(ks[1], (BATCH, D), f32),
        "ctx": nrm(ks[2], (BATCH, CTX_LEN, D), f32),
        "c_ctx": nrm(ks[3], (D,), f32),
        "w_mod": nrm(ks[4], (DEPTH, D, N_MOD * D), f32) * (0.5 * D ** -0.5),
        "b_mod": nrm(ks[5], (DEPTH, N_MOD * D), f32) * 0.02,
        "norm_mix": 1.0 + 0.02 * nrm(ks[6], (DEPTH, D), f32),
        "norm_ffn": 1.0 + 0.02 * nrm(ks[7], (DEPTH, D), f32),
        "w_in": nrm(ks[8], (DEPTH, D, IN_WIDTH), f32) * D ** -0.5,
        "w_out": nrm(ks[9], (DEPTH, D_MIX, D), f32) * D_MIX ** -0.5,
        "pool_w": nrm(ks[10], (DEPTH, len(POOL_WINDOWS), POOL_GROUP, POOL_GROUP), f32) * POOL_GROUP ** -0.5,
        "pool_scale": 1.0 + 0.1 * nrm(ks[11], (DEPTH, POOL_WIDTH), f32),
        "ssm_a_re": -0.5 + 0.01 * nrm(ks[12], (DEPTH, 2, SSM_GROUPS, SSM_STATE), f32),
        "ssm_a_im": a_im0 + 0.01 * nrm(ks[13], (DEPTH, 2, SSM_GROUPS, SSM_STATE), f32),
        "ssm_log_dt": jax.random.uniform(ks[14], (DEPTH, 2, SSM_GROUPS), f32,
                                         math.log(1e-3), math.log(1e-1)),
        "ssm_b_re": nrm(ks[15], (DEPTH, 2, SSM_GROUPS, SSM_STATE, SSM_GROUP_CH), f32) * (0.5 / SSM_GROUP_CH) ** 0.5,
        "ssm_b_im": nrm(ks[16], (DEPTH, 2, SSM_GROUPS, SSM_STATE, SSM_GROUP_CH), f32) * (0.5 / SSM_GROUP_CH) ** 0.5,
        "ssm_c_re": nrm(ks[17], (DEPTH, 2, SSM_GROUPS, SSM_GROUP_CH, SSM_STATE), f32) * (0.5 / SSM_STATE) ** 0.5,
        "ssm_c_im": nrm(ks[18], (DEPTH, 2, SSM_GROUPS, SSM_GROUP_CH, SSM_STATE), f32) * (0.5 / SSM_STATE) ** 0.5,
        "ssm_d": nrm(ks[19], (DEPTH, SSM_WIDTH), f32),
        "ssm_glu_w": nrm(ks[20], (DEPTH, SSM_WIDTH, SSM_WIDTH), f32) * SSM_WIDTH ** -0.5,
        "ssm_glu_b": 0.02 * nrm(ks[21], (DEPTH, SSM_WIDTH), f32),
        "q_norm": 1.0 + 0.02 * nrm(ks[22], (DEPTH, HEAD_DIM), f32),
        "k_norm": 1.0 + 0.02 * nrm(ks[23], (DEPTH, HEAD_DIM), f32),
        "attn_sink": 0.5 * nrm(ks[24], (DEPTH, N_HEADS), f32),
        "ffn_w_gate": nrm(ks[25], (DEPTH, D, D_FF), f32) * D ** -0.5,
        "ffn_w_up": nrm(ks[26], (DEPTH, D, D_FF), f32) * D ** -0.5,
        "ffn_w_down": nrm(ks[27], (DEPTH, D_FF, D), f32) * D_FF ** -0.5,
    }


def reference(x, c, ctx, c_ctx, w_mod, b_mod, norm_mix, norm_ffn, w_in, w_out, pool_w, pool_scale,
              ssm_a_re, ssm_a_im, ssm_log_dt, ssm_b_re, ssm_b_im, ssm_c_re, ssm_c_im, ssm_d,
              ssm_glu_w, ssm_glu_b, q_norm, k_norm, attn_sink, ffn_w_gate, ffn_w_up, ffn_w_down):
    L = x.shape[1]
    cos, sin = _axial_rope_tables(L)
    silu_c = jax.nn.silu(c)
    silu_cc = jax.nn.silu(c_ctx)
    for l in range(DEPTH):
        need_ctx = l < DEPTH - 1
        mx = silu_c @ w_mod[l] + b_mod[l]
        mc = silu_cc @ w_mod[l] + b_mod[l]
        sh1, sc1, g1, sh2, sc2, g2 = jnp.split(mx[:, None, :], N_MOD, axis=-1)
        csh1, csc1, cg1, csh2, csc2, cg2 = jnp.split(mc, N_MOD, axis=-1)

        hx = _rmsnorm(x, norm_mix[l]) * (1.0 + sc1) + sh1
        hc = _rmsnorm(ctx, norm_mix[l]) * (1.0 + csc1) + csh1
        mix_x, mix_c = _mixing(hx, hc, w_in[l], w_out[l], pool_w[l], pool_scale[l],
                               ssm_a_re[l], ssm_a_im[l], ssm_log_dt[l], ssm_b_re[l], ssm_b_im[l],
                               ssm_c_re[l], ssm_c_im[l], ssm_d[l], ssm_glu_w[l], ssm_glu_b[l],
                               q_norm[l], k_norm[l], attn_sink[l], cos, sin, need_ctx)
        x = x + g1 * mix_x
        hx2 = _rmsnorm(x, norm_ffn[l]) * (1.0 + sc2) + sh2
        x = x + g2 * _swiglu(hx2, ffn_w_gate[l], ffn_w_up[l], ffn_w_down[l])

        if need_ctx:
            ctx = ctx + cg1 * mix_c
            hc2 = _rmsnorm(ctx, norm_ffn[l]) * (1.0 + csc2) + csh2
            ctx = ctx + cg2 * _swiglu(hc2, ffn_w_gate[l], ffn_w_up[l], ffn_w_down[l])
    return x
```

```python
import functools
import math

import jax
import jax.numpy as jnp
from jax import lax
from jax.experimental import pallas as pl
from jax.experimental.pallas import tpu as pltpu

f32 = jnp.float32
bf16 = jnp.bfloat16

GRID_W = 64
POOL_WINDOWS = (2, 4, 8, 16)
POOL_GROUP = 64
POOL_WIDTH = POOL_GROUP * len(POOL_WINDOWS)
SSM_GROUP_CH = 16
SSM_GROUPS = 16
SSM_WIDTH = SSM_GROUP_CH * SSM_GROUPS
SSM_STATE = 64
SSM_LANES = SSM_GROUPS * SSM_STATE
HEAD_DIM = 64
N_HEADS = 8
N_KV_HEADS = 2
GROUP = N_HEADS // N_KV_HEADS
ATTN_WIDTH = N_HEADS * HEAD_DIM
KV_WIDTH = N_KV_HEADS * HEAD_DIM
WINDOW = 128
ROPE_BASE = 10000.0
ROPE_FREQS = HEAD_DIM // 4
N_MOD = 6
EPS = 1e-6

LANES = 128
SUBLANES = 8
TOKEN_TILE = 256
Q_TILE = 128
SCAN_STEPS = 64
VMEM_LIMIT_BYTES = 56 * 1024 * 1024


def _sigmoid(x):
    return 1.0 / (1.0 + jnp.exp(-x))


def _silu(x):
    return x * _sigmoid(x)


def _gelu_tanh(x):
    return 0.5 * x * (1.0 + jnp.tanh(math.sqrt(2.0 / math.pi) * (x + 0.044715 * (x * x * x))))


def _dot(a, b):
    return jnp.dot(a, b, preferred_element_type=f32)


def _dot_nt(a, b):
    return lax.dot_general(a, b, (((1,), (1,)), ((), ())), preferred_element_type=f32)


def _const_spec(shape):
    nd = len(shape)
    return pl.BlockSpec(shape, lambda *_: (0,) * nd, pipeline_mode=pl.Buffered(1))


def _mod_kernel(c_ref, w_ref, b_ref, o_ref):
    a = _silu(c_ref[...]).astype(bf16)
    o_ref[...] = _dot(a, w_ref[...].astype(bf16)) + b_ref[...]


def _modulation(c16, w_mod, b_mod):
    depth, d, n = w_mod.shape
    tn = 1536
    return pl.pallas_call(
        _mod_kernel,
        out_shape=jax.ShapeDtypeStruct((depth, 16, n), f32),
        grid=(depth, n // tn),
        in_specs=[
            pl.BlockSpec((16, d), lambda l, j: (0, 0)),
            pl.BlockSpec((None, d, tn), lambda l, j: (l, 0, j)),
            pl.BlockSpec((None, 1, tn), lambda l, j: (l, 0, j)),
        ],
        out_specs=pl.BlockSpec((None, 16, tn), lambda l, j: (l, 0, j)),
        compiler_params=pltpu.CompilerParams(dimension_semantics=("arbitrary", "arbitrary"),
                                             vmem_limit_bytes=VMEM_LIMIT_BYTES),
        name="modulation",
    )(c16, w_mod, b_mod.reshape(depth, 1, n))


def _rmsnorm_mod(x, g, scale, shift):
    y = x * lax.rsqrt(jnp.mean(x * x, axis=-1, keepdims=True) + EPS) * g
    return y * (1.0 + scale) + shift


def _rope(x, cos, sin_signed):
    lane = lax.broadcasted_iota(jnp.int32, x.shape, 1)
    first = (lane % 32) < 16
    partner = jnp.where(first, pltpu.roll(x, LANES - 16, 1), pltpu.roll(x, 16, 1))
    return x * cos + partner * sin_signed


def _inproj_kernel(x_ref, mod_ref, g_ref, w_ref, cos_ref, sin_ref, qn_ref, kn_ref, ones_ref,
                   pool_ref, ssm_ref, q_ref, k2_ref, v2_ref):
    h = _rmsnorm_mod(x_ref[...], g_ref[...], mod_ref[1:2, :], mod_ref[0:1, :]).astype(bf16)
    proj = _dot(h, w_ref[...])
    pool_ref[...] = proj[:, 0:POOL_WIDTH]
    ssm_ref[...] = proj[:, POOL_WIDTH:POOL_WIDTH + SSM_WIDTH]
    o = POOL_WIDTH + SSM_WIDTH
    q = proj[:, o:o + ATTN_WIDTH]
    k = proj[:, o + ATTN_WIDTH:o + ATTN_WIDTH + KV_WIDTH]
    v = proj[:, o + ATTN_WIDTH + KV_WIDTH:o + ATTN_WIDTH + 2 * KV_WIDTH]
    cos = cos_ref[...]
    sin = sin_ref[...]
    q_ms = _dot((q * q).astype(bf16), ones_ref[...])
    qn = q * lax.rsqrt(q_ms + EPS) * qn_ref[...]
    for cblk in range(ATTN_WIDTH // LANES):
        sl = slice(cblk * LANES, (cblk + 1) * LANES)
        q_ref[:, sl] = _rope(qn[:, sl], cos, sin).astype(bf16)
    k_ms = _dot((k * k).astype(bf16), ones_ref[0:KV_WIDTH, 0:KV_WIDTH])
    kr = _rope(k * lax.rsqrt(k_ms + EPS) * kn_ref[...], cos, sin)
    lane = lax.broadcasted_iota(jnp.int32, kr.shape, 1)
    left = lane < HEAD_DIM
    ksw = pltpu.roll(kr, HEAD_DIM, 1)
    k2_ref[:, 0:LANES] = jnp.where(left, kr, ksw).astype(bf16)
    k2_ref[:, LANES:2 * LANES] = jnp.where(left, ksw, kr).astype(bf16)
    vsw = pltpu.roll(v, HEAD_DIM, 1)
    zero = jnp.zeros_like(v)
    v2_ref[:, 0:LANES] = jnp.where(left, v, zero).astype(bf16)
    v2_ref[:, LANES:2 * LANES] = jnp.where(left, zero, vsw).astype(bf16)
    v2_ref[:, 2 * LANES:3 * LANES] = jnp.where(left, vsw, zero).astype(bf16)
    v2_ref[:, 3 * LANES:4 * LANES] = jnp.where(left, zero, v).astype(bf16)


def _mod_row(b, j, n_batch):
    return jnp.where(j == 0, n_batch, b)


def _inproj(xc, mod_l, g, w_in, cos_t, sin_t, qn, kn, ones):
    B, NT, D = xc.shape
    T = TOKEN_TILE
    nj = NT // T
    win = w_in.shape[1]
    tok = lambda w: pl.BlockSpec((None, T, w), lambda b, j: (b, j, 0))
    return pl.pallas_call(
        _inproj_kernel,
        out_shape=(
            jax.ShapeDtypeStruct((B, NT, POOL_WIDTH), f32),
            jax.ShapeDtypeStruct((NT, B * SSM_WIDTH), f32),
            jax.ShapeDtypeStruct((B, NT, ATTN_WIDTH), bf16),
            jax.ShapeDtypeStruct((B, NT, 2 * KV_WIDTH), bf16),
            jax.ShapeDtypeStruct((B, NT, 4 * KV_WIDTH), bf16),
        ),
        grid=(B, nj),
        in_specs=[
            tok(D),
            pl.BlockSpec((None, N_MOD, D), lambda b, j: (_mod_row(b, j, B), 0, 0)),
            _const_spec((1, D)),
            _const_spec((D, win)),
            pl.BlockSpec((T, LANES), lambda b, j: (j, 0)),
            pl.BlockSpec((T, LANES), lambda b, j: (j, 0)),
            _const_spec((1, ATTN_WIDTH)),
            _const_spec((1, KV_WIDTH)),
            _const_spec((ATTN_WIDTH, ATTN_WIDTH)),
        ],
        out_specs=(
            tok(POOL_WIDTH),
            pl.BlockSpec((T, SSM_WIDTH), lambda b, j: (j, b)),
            tok(ATTN_WIDTH),
            tok(2 * KV_WIDTH),
            tok(4 * KV_WIDTH),
        ),
        compiler_params=pltpu.CompilerParams(dimension_semantics=("arbitrary", "arbitrary"),
                                             vmem_limit_bytes=VMEM_LIMIT_BYTES),
        name="inproj",
    )(xc, mod_l, g, w_in, cos_t, sin_t, qn, kn, ones)


def _shift_rows(x, s, row):
    n = x.shape[0]
    if s > 0:
        return jnp.where(row >= s, pltpu.roll(x, s, 0), 0.0)
    return jnp.where(row < n + s, pltpu.roll(x, n + s, 0), 0.0)


def _window_sums(u, half_a, half_b, lane_split):
    row = lax.broadcasted_iota(jnp.int32, u.shape, 0)
    lane = lax.broadcasted_iota(jnp.int32, u.shape, 1)
    bwd = u
    fwd = u
    k = 1
    out_a = None
    while True:
        if k == half_a:
            out_a = _shift_rows(bwd, 1, row) + fwd
        if k == half_b:
            out_b = _shift_rows(bwd, 1, row) + fwd
            break
        bwd = bwd + _shift_rows(bwd, k, row)
        fwd = fwd + _shift_rows(fwd, -k, row)
        k *= 2
    n = u.shape[0]
    half = jnp.where(lane < lane_split, half_a, half_b)
    cnt = jnp.minimum(row + half, n) - jnp.maximum(row - half, 0)
    return jnp.where(lane < lane_split, out_a, out_b) / cnt.astype(f32)


def _pool_segment(u_ref, w_ref, scale_ref, o_ref, r0, n):
    parts = []
    for blk in range(POOL_WIDTH // LANES):
        u = u_ref[r0:r0 + n, blk * LANES:(blk + 1) * LANES]
        wa, wb = POOL_WINDOWS[2 * blk], POOL_WINDOWS[2 * blk + 1]
        mean = _window_sums(u, wa // 2, wb // 2, POOL_GROUP)
        parts.append((mean - u).astype(bf16))
    d = jnp.concatenate(parts, axis=1)
    o_ref[r0:r0 + n, :] = (_dot(d, w_ref[...]) * scale_ref[...]).astype(bf16)


def _pool_kernel(u_ref, w_ref, scale_ref, o_ref, *, n_ctx):
    nt = u_ref.shape[0]
    _pool_segment(u_ref, w_ref, scale_ref, o_ref, 0, n_ctx)
    _pool_segment(u_ref, w_ref, scale_ref, o_ref, n_ctx, nt - n_ctx)


def _pool(pool_u, w_bd, scale, n_ctx):
    B, NT, W = pool_u.shape
    return pl.pallas_call(
        functools.partial(_pool_kernel, n_ctx=n_ctx),
        out_shape=jax.ShapeDtypeStruct((B, NT, W), bf16),
        grid=(B,),
        in_specs=[
            pl.BlockSpec((None, NT, W), lambda b: (b, 0, 0)),
            _const_spec((W, W)),
            _const_spec((1, W)),
        ],
        out_specs=pl.BlockSpec((None, NT, W), lambda b: (b, 0, 0)),
        compiler_params=pltpu.CompilerParams(dimension_semantics=("arbitrary",),
                                             vmem_limit_bytes=VMEM_LIMIT_BYTES),
        name="pool",
    )(pool_u, w_bd, scale)


def _ssm_kernel(u_ref, bmat_ref, cmat_ref, are_ref, aim_ref, y_ref, bu_ref, s_ref, state_ref):
    d = pl.program_id(0)
    j = pl.program_id(1)
    steps = u_ref.shape[0] // SUBLANES

    @pl.when(j == 0)
    def _():
        state_ref[...] = jnp.zeros_like(state_ref)

    bu_ref[...] = _dot(u_ref[...].astype(bf16), bmat_ref[...])
    a_re = are_ref[...]
    a_im = aim_ref[...]

    def body(t, carry):
        s_re, s_im = carry
        tt = jnp.where(d == 0, t, steps - 1 - t)
        r = pl.multiple_of(tt * SUBLANES, SUBLANES)
        b_re = bu_ref[pl.ds(r, SUBLANES), 0:SSM_LANES]
        b_im = bu_ref[pl.ds(r, SUBLANES), SSM_LANES:2 * SSM_LANES]
        n_re = a_re * s_re - a_im * s_im + b_re
        n_im = a_re * s_im + a_im * s_re + b_im
        s_ref[pl.ds(r, SUBLANES), 0:SSM_LANES] = n_re
        s_ref[pl.ds(r, SUBLANES), SSM_LANES:2 * SSM_LANES] = n_im
        return n_re, n_im

    s_re, s_im = lax.fori_loop(0, steps, body,
                               (state_ref[:, 0:SSM_LANES], state_ref[:, SSM_LANES:2 * SSM_LANES]),
                               unroll=2)
    state_ref[:, 0:SSM_LANES] = s_re
    state_ref[:, SSM_LANES:2 * SSM_LANES] = s_im
    y_ref[...] = _dot(s_ref[...].astype(bf16), cmat_ref[...])


def _ssm(u_tm, bmat, cmat, a_re, a_im, n_batch, n_ctx):
    rows, W = u_tm.shape
    R = SCAN_STEPS * n_batch
    n = rows // R
    nc = n_ctx // SCAN_STEPS

    def chunk(d, j):
        return jnp.where(d == 0, j, jnp.where(j < nc, nc - 1 - j, n + nc - 1 - j))

    return pl.pallas_call(
        _ssm_kernel,
        out_shape=jax.ShapeDtypeStruct((2, rows, W), f32),
        grid=(2, n),
        in_specs=[
            pl.BlockSpec((R, W), lambda d, j: (chunk(d, j), 0)),
            pl.BlockSpec((None, W, 2 * SSM_LANES), lambda d, j: (d, 0, 0)),
            pl.BlockSpec((None, 2 * SSM_LANES, W), lambda d, j: (d, 0, 0)),
            pl.BlockSpec((None, n_batch, SSM_LANES), lambda d, j: (d, 0, 0)),
            pl.BlockSpec((None, n_batch, SSM_LANES), lambda d, j: (d, 0, 0)),
        ],
        out_specs=pl.BlockSpec((None, R, W), lambda d, j: (d, chunk(d, j), 0)),
        scratch_shapes=[
            pltpu.VMEM((R, 2 * SSM_LANES), f32),
            pltpu.VMEM((R, 2 * SSM_LANES), f32),
            pltpu.VMEM((n_batch, 2 * SSM_LANES), f32),
        ],
        compiler_params=pltpu.CompilerParams(dimension_semantics=("arbitrary", "arbitrary"),
                                             vmem_limit_bytes=VMEM_LIMIT_BYTES),
        name="ssm_scan",
    )(u_tm, bmat, cmat, a_re, a_im)


def _attend_pair(qp, keys, vals_l, vals_r, masks, sink_a, sink_b):
    nq = qp.shape[0]
    lane = lax.broadcasted_iota(jnp.int32, qp.shape, 1)
    zero = jnp.zeros_like(qp)
    lhs = jnp.concatenate([jnp.where(lane < HEAD_DIM, qp, zero),
                           jnp.where(lane < HEAD_DIM, zero, qp)], axis=0)
    row = lax.broadcasted_iota(jnp.int32, (2 * nq, 1), 0)
    sink = jnp.where(row < nq, sink_a, sink_b)
    scores = []
    m = sink
    for kk, mk in zip(keys, masks):
        s = _dot_nt(lhs, kk)
        if mk is not None:
            s = jnp.where(jnp.concatenate([mk, mk], axis=0), s, -jnp.inf)
        scores.append(s)
        m = jnp.maximum(m, jnp.max(s, axis=-1, keepdims=True))
    den = jnp.exp(sink - m)
    out_a = jnp.zeros((nq, LANES), f32)
    out_b = jnp.zeros((nq, LANES), f32)
    for s, vl, vr in zip(scores, vals_l, vals_r):
        p = jnp.exp(s - m)
        den = den + jnp.sum(p, axis=-1, keepdims=True)
        pb = p.astype(bf16)
        out_a = out_a + _dot(pb[0:nq], vl)
        out_b = out_b + _dot(pb[nq:2 * nq], vr)
    return out_a / den[0:nq] + out_b / den[nq:2 * nq]


def _attn_kernel(sink_ref, q_ref, k2_ref, v2_ref, o_ref, *, n_ctx):
    i = pl.program_id(1)
    nq = q_ref.shape[0]
    n_lat = k2_ref.shape[0] - n_ctx
    n_ctx_tiles = n_ctx // nq
    span = 3 * WINDOW

    def run(keys_rows, masks):
        for hk in range(N_KV_HEADS):
            keys = [k2_ref[r, hk * LANES:(hk + 1) * LANES] for r in keys_rows]
            vals_l = [v2_ref[r, 2 * hk * LANES:(2 * hk + 1) * LANES] for r in keys_rows]
            vals_r = [v2_ref[r, (2 * hk + 1) * LANES:(2 * hk + 2) * LANES] for r in keys_rows]
            for p in range(GROUP // 2):
                c0 = (hk * (GROUP // 2) + p) * LANES
                h_a = hk * GROUP + 2 * p
                out = _attend_pair(q_ref[:, c0:c0 + LANES], keys, vals_l, vals_r, masks,
                                   sink_ref[h_a], sink_ref[h_a + 1])
                o_ref[:, c0:c0 + LANES] = out.astype(bf16)

    ctx_rows = pl.ds(0, n_ctx)

    @pl.when(i < n_ctx_tiles)
    def _():
        run([ctx_rows], [None])

    @pl.when(i >= n_ctx_tiles)
    def _():
        q0 = (i - n_ctx_tiles) * nq
        start = jnp.clip(q0 - WINDOW, 0, n_lat - span)
        q_pos = q0 + lax.broadcasted_iota(jnp.int32, (nq, span), 0)
        k_pos = start + lax.broadcasted_iota(jnp.int32, (nq, span), 1)
        valid = jnp.abs(k_pos - q_pos) <= WINDOW
        win_rows = pl.ds(pl.multiple_of(n_ctx + start, WINDOW), span)
        run([win_rows, ctx_rows], [valid, None])


def _attention(sink, q, k2, v2, n_ctx):
    B, NT, _ = q.shape
    return pl.pallas_call(
        functools.partial(_attn_kernel, n_ctx=n_ctx),
        out_shape=jax.ShapeDtypeStruct((B, NT, ATTN_WIDTH), bf16),
        grid=(B, NT // Q_TILE),
        in_specs=[
            pl.BlockSpec(memory_space=pltpu.SMEM),
            pl.BlockSpec((None, Q_TILE, ATTN_WIDTH), lambda b, i: (b, i, 0)),
            pl.BlockSpec((None, NT, 2 * KV_WIDTH), lambda b, i: (b, 0, 0)),
            pl.BlockSpec((None, NT, 4 * KV_WIDTH), lambda b, i: (b, 0, 0)),
        ],
        out_specs=pl.BlockSpec((None, Q_TILE, ATTN_WIDTH), lambda b, i: (b, i, 0)),
        compiler_params=pltpu.CompilerParams(dimension_semantics=("arbitrary", "arbitrary"),
                                             vmem_limit_bytes=VMEM_LIMIT_BYTES),
        name="attention",
    )(sink, q, k2, v2)


def _outffn_kernel(x_ref, mod_ref, po_ref, u_ref, y0_ref, y1_ref, ao_ref, d_ref, gw_ref, gb_ref,
                   wo_ref, g_ref, wg_ref, wu_ref, wd_ref, o_ref, *, ff_chunk):
    y = d_ref[...] * u_ref[...] + y0_ref[...] + y1_ref[...]
    ge = _gelu_tanh(y)
    so = ge * _sigmoid(_dot(ge.astype(bf16), gw_ref[...]) + gb_ref[...])
    o1 = POOL_WIDTH
    o2 = POOL_WIDTH + SSM_WIDTH
    mix = (_dot(po_ref[...], wo_ref[0:o1, :]) + _dot(so.astype(bf16), wo_ref[o1:o2, :])
           + _dot(ao_ref[...], wo_ref[o2:, :]))
    x1 = x_ref[...] + mod_ref[2:3, :] * mix
    h = _rmsnorm_mod(x1, g_ref[...], mod_ref[4:5, :], mod_ref[3:4, :]).astype(bf16)
    d_ff = wg_ref.shape[1]
    acc = jnp.zeros_like(x1)
    for c0 in range(0, d_ff, ff_chunk):
        gate = _dot(h, wg_ref[:, c0:c0 + ff_chunk])
        up = _dot(h, wu_ref[:, c0:c0 + ff_chunk])
        acc = acc + _dot((_silu(gate) * up).astype(bf16), wd_ref[c0:c0 + ff_chunk, :])
    o_ref[...] = x1 + mod_ref[5:6, :] * acc


def _outffn(xc, mod_l, po, u_tm, y, ao, d_skip, glu_w, glu_b, w_out, g, w_gate, w_up, w_down,
            skip_ctx):
    B, NT, D = xc.shape
    T = TOKEN_TILE
    j0 = 1 if skip_ctx else 0
    nj = NT // T - j0
    d_ff = w_gate.shape[1]
    tok = lambda w: pl.BlockSpec((None, T, w), lambda b, j: (b, j + j0, 0))
    y3 = y.reshape(2, NT, B * SSM_WIDTH)
    return pl.pallas_call(
        functools.partial(_outffn_kernel, ff_chunk=d_ff // 2),
        out_shape=jax.ShapeDtypeStruct((B, nj * T, D), f32),
        grid=(B, nj),
        in_specs=[
            tok(D),
            pl.BlockSpec((None, N_MOD, D), lambda b, j: (_mod_row(b, j + j0, B), 0, 0)),
            tok(POOL_WIDTH),
            pl.BlockSpec((T, SSM_WIDTH), lambda b, j: (j + j0, b)),
            pl.BlockSpec((None, T, SSM_WIDTH), lambda b, j: (0, j + j0, b)),
            pl.BlockSpec((None, T, SSM_WIDTH), lambda b, j: (1, j + j0, b)),
            tok(ATTN_WIDTH),
            _const_spec((1, SSM_WIDTH)),
            _const_spec((SSM_WIDTH, SSM_WIDTH)),
            _const_spec((1, SSM_WIDTH)),
            _const_spec((D, D)),
            _const_spec((1, D)),
            _const_spec((D, d_ff)),
            _const_spec((D, d_ff)),
            _const_spec((d_ff, D)),
        ],
        out_specs=pl.BlockSpec((None, T, D), lambda b, j: (b, j, 0)),
        compiler_params=pltpu.CompilerParams(dimension_semantics=("arbitrary", "arbitrary"),
                                             vmem_limit_bytes=VMEM_LIMIT_BYTES),
        name="outproj_ffn",
    )(xc, mod_l, po, u_tm, y3, y3, ao, d_skip, glu_w, glu_b, w_out, g, w_gate, w_up, w_down)


def _rope_tables(n_ctx, n_lat):
    pos = jnp.arange(n_lat)
    row = (pos // GRID_W).astype(f32)
    col = (pos % GRID_W).astype(f32)
    inv = jnp.power(ROPE_BASE, -jnp.arange(ROPE_FREQS, dtype=f32) / ROPE_FREQS)
    lane = jnp.arange(LANES)
    dim = lane % HEAD_DIM
    freq = inv[dim % ROPE_FREQS]
    ang = jnp.where((dim // (2 * ROPE_FREQS)) == 0, row[:, None], col[:, None]) * freq[None, :]
    sign = jnp.where((dim % (2 * ROPE_FREQS)) < ROPE_FREQS, -1.0, 1.0).astype(f32)
    cos = jnp.concatenate([jnp.ones((n_ctx, LANES), f32), jnp.cos(ang)], axis=0)
    sin = jnp.concatenate([jnp.zeros((n_ctx, LANES), f32), jnp.sin(ang) * sign], axis=0)
    return cos, sin


def _block_diag(blocks):
    g, r, c = blocks.shape
    eye = jnp.eye(g, dtype=blocks.dtype)
    return jnp.einsum('grc,gh->grhc', blocks, eye).reshape(g * r, g * c)


def _ssm_matrices(a_re, a_im, log_dt, b_re, b_im, c_re, c_im, n_batch):
    lam = lax.complex(jnp.minimum(a_re.astype(f32), -1e-4), a_im.astype(f32))
    dt = jnp.exp(log_dt.astype(f32))[..., None]
    a_bar = jnp.exp(lam * dt)
    b_bar = ((a_bar - 1.0) / lam)[..., None] * lax.complex(b_re.astype(f32), b_im.astype(f32))
    bmat, cmat = [], []
    for d in range(2):
        br = _block_diag(jnp.swapaxes(jnp.real(b_bar[d]), 1, 2))
        bi = _block_diag(jnp.swapaxes(jnp.imag(b_bar[d]), 1, 2))
        bmat.append(jnp.concatenate([br, bi], axis=1))
        cr = _block_diag(jnp.swapaxes(c_re[d].astype(f32), 1, 2))
        ci = _block_diag(jnp.swapaxes(c_im[d].astype(f32), 1, 2))
        cmat.append(jnp.concatenate([cr, -ci], axis=0))
    a_re_b = jnp.broadcast_to(jnp.real(a_bar).reshape(2, 1, SSM_LANES), (2, n_batch, SSM_LANES))
    a_im_b = jnp.broadcast_to(jnp.imag(a_bar).reshape(2, 1, SSM_LANES), (2, n_batch, SSM_LANES))
    return jnp.stack(bmat).astype(bf16), jnp.stack(cmat).astype(bf16), a_re_b, a_im_b


def kernel(x, c, ctx, c_ctx, w_mod, b_mod, norm_mix, norm_ffn, w_in, w_out, pool_w, pool_scale, ssm_a_re, ssm_a_im, ssm_log_dt, ssm_b_re, ssm_b_im, ssm_c_re, ssm_c_im, ssm_d, ssm_glu_w, ssm_glu_b, q_norm, k_norm, attn_sink, ffn_w_gate, ffn_w_up, ffn_w_down):
    B, L, D = x.shape
    Lc = ctx.shape[1]
    depth = w_mod.shape[0]
    assert Lc == TOKEN_TILE and L % TOKEN_TILE == 0 and L % GRID_W == 0 and L >= 3 * WINDOW
    assert B == SUBLANES and w_in.shape[2] == POOL_WIDTH + SSM_WIDTH + ATTN_WIDTH + 2 * KV_WIDTH
    assert w_mod.shape[2] == N_MOD * D and Lc % SCAN_STEPS == 0 and L % SCAN_STEPS == 0

    c16 = jnp.concatenate([c, c_ctx[None, :], jnp.zeros((16 - B - 1, D), f32)], axis=0)
    mod = _modulation(c16, w_mod, b_mod).reshape(depth, 16, N_MOD, D)

    cos_t, sin_t = _rope_tables(Lc, L)
    head = jnp.arange(ATTN_WIDTH) // HEAD_DIM
    ones = jnp.where(head[:, None] == head[None, :], 1.0 / HEAD_DIM, 0.0).astype(bf16)

    xc = jnp.concatenate([ctx, x], axis=1)
    for l in range(depth):
        last = l == depth - 1
        qn = (jnp.tile(q_norm[l], N_HEADS) * HEAD_DIM ** -0.5).reshape(1, ATTN_WIDTH)
        kn = jnp.tile(k_norm[l], N_KV_HEADS).reshape(1, KV_WIDTH)
        pool_u, u_tm, q, k2, v2 = _inproj(xc, mod[l], norm_mix[l].reshape(1, D), w_in[l].astype(bf16),
                                          cos_t, sin_t, qn, kn, ones)
        po = _pool(pool_u, _block_diag(pool_w[l]).astype(bf16), pool_scale[l].reshape(1, POOL_WIDTH), Lc)
        bmat, cmat, a_re, a_im = _ssm_matrices(ssm_a_re[l], ssm_a_im[l], ssm_log_dt[l], ssm_b_re[l],
                                               ssm_b_im[l], ssm_c_re[l], ssm_c_im[l], B)
        u_rows = u_tm.reshape((Lc + L) * B, SSM_WIDTH)
        y = _ssm(u_rows, bmat, cmat, a_re, a_im, B, Lc)
        ao = _attention(attn_sink[l], q, k2, v2, Lc)
        xc = _outffn(xc, mod[l], po, u_tm, y, ao, ssm_d[l].reshape(1, SSM_WIDTH),
                     ssm_glu_w[l].astype(bf16), ssm_glu_b[l].reshape(1, SSM_WIDTH),
                     w_out[l].astype(bf16), norm_ffn[l].reshape(1, D), ffn_w_gate[l].astype(bf16),
                     ffn_w_up[l].astype(bf16), ffn_w_down[l].astype(bf16), skip_ctx=last)
    return xc
```

```python
import functools
import math

import jax
import jax.numpy as jnp
from jax import lax
from jax.experimental import pallas as pl
from jax.experimental.pallas import tpu as pltpu

f32 = jnp.float32
bf16 = jnp.bfloat16

GRID_W = 64
POOL_WINDOWS = (2, 4, 8, 16)
POOL_GROUP = 64
POOL_WIDTH = POOL_GROUP * len(POOL_WINDOWS)
SSM_GROUP_CH = 16
SSM_GROUPS = 16
SSM_WIDTH = SSM_GROUP_CH * SSM_GROUPS
SSM_STATE = 64
SSM_LANES = SSM_GROUPS * SSM_STATE
HEAD_DIM = 64
N_HEADS = 8
N_KV_HEADS = 2
GROUP = N_HEADS // N_KV_HEADS
ATTN_WIDTH = N_HEADS * HEAD_DIM
KV_WIDTH = N_KV_HEADS * HEAD_DIM
WINDOW = 128
ROPE_BASE = 10000.0
ROPE_FREQS = HEAD_DIM // 4
N_MOD = 6
EPS = 1e-6

LANES = 128
SUBLANES = 8
TOKEN_TILE = 256
Q_TILE = 128
SCAN_STEPS = 64
SCAN_PITCH = SCAN_STEPS + 4
VMEM_LIMIT_BYTES = 56 * 1024 * 1024


def _sigmoid(x):
    return 1.0 / (1.0 + jnp.exp(-x))


def _silu(x):
    return x * _sigmoid(x)


def _gelu_tanh(x):
    return 0.5 * x * (1.0 + jnp.tanh(math.sqrt(2.0 / math.pi) * (x + 0.044715 * (x * x * x))))


def _dot(a, b):
    return jnp.dot(a, b, preferred_element_type=f32)


def _dot_nt(a, b):
    return lax.dot_general(a, b, (((1,), (1,)), ((), ())), preferred_element_type=f32)


def _const_spec(shape):
    nd = len(shape)
    return pl.BlockSpec(shape, lambda *_: (0,) * nd, pipeline_mode=pl.Buffered(1))


def _layer_spec(shape, layer):
    nd = len(shape)
    return pl.BlockSpec((None,) + tuple(shape), lambda *_: (layer,) + (0,) * nd,
                        pipeline_mode=pl.Buffered(1))


def _params(*sem):
    return pltpu.CompilerParams(dimension_semantics=sem, vmem_limit_bytes=VMEM_LIMIT_BYTES)


def _mod_kernel(c_ref, w_ref, b_ref, o_ref):
    a = _silu(c_ref[...]).astype(bf16)
    o_ref[...] = _dot(a, w_ref[...].astype(bf16)) + b_ref[...]


def _modulation(c16, w_mod, b_mod):
    depth, d, n = w_mod.shape
    tn = 1536
    return pl.pallas_call(
        _mod_kernel,
        out_shape=jax.ShapeDtypeStruct((depth, 16, n), f32),
        grid=(depth, n // tn),
        in_specs=[
            pl.BlockSpec((16, d), lambda l, j: (0, 0)),
            pl.BlockSpec((None, d, tn), lambda l, j: (l, 0, j)),
            pl.BlockSpec((None, 1, tn), lambda l, j: (l, 0, j)),
        ],
        out_specs=pl.BlockSpec((None, 16, tn), lambda l, j: (l, 0, j)),
        compiler_params=_params("arbitrary", "arbitrary"),
        name="modulation",
    )(c16, w_mod, b_mod.reshape(depth, 1, n))


def _rmsnorm_mod(x, g, scale, shift):
    y = x * lax.rsqrt(jnp.mean(x * x, axis=-1, keepdims=True) + EPS) * g
    return y * (1.0 + scale) + shift


def _rope(x, cos, sin_signed):
    lane = lax.broadcasted_iota(jnp.int32, x.shape, 1)
    first = (lane % 32) < 16
    partner = jnp.where(first, pltpu.roll(x, LANES - 16, 1), pltpu.roll(x, 16, 1))
    return x * cos + partner * sin_signed


def _inproj_kernel(x_ref, mod_ref, g_ref, w_ref, cos_ref, sin_ref, qn_ref, kn_ref, ones_ref,
                   pool_ref, ssm_ref, q_ref, k2_ref, v2_ref):
    h = _rmsnorm_mod(x_ref[...], g_ref[...], mod_ref[1:2, :], mod_ref[0:1, :]).astype(bf16)
    proj = _dot(h, w_ref[...])
    pool_ref[...] = proj[:, 0:POOL_WIDTH]
    ssm_ref[...] = proj[:, POOL_WIDTH:POOL_WIDTH + SSM_WIDTH]
    o = POOL_WIDTH + SSM_WIDTH
    q = proj[:, o:o + ATTN_WIDTH]
    k = proj[:, o + ATTN_WIDTH:o + ATTN_WIDTH + KV_WIDTH]
    v = proj[:, o + ATTN_WIDTH + KV_WIDTH:o + ATTN_WIDTH + 2 * KV_WIDTH]
    cos = cos_ref[...]
    sin = sin_ref[...]
    q_ms = _dot((q * q).astype(bf16), ones_ref[...])
    qn = q * lax.rsqrt(q_ms + EPS) * qn_ref[...]
    for cblk in range(ATTN_WIDTH // LANES):
        sl = slice(cblk * LANES, (cblk + 1) * LANES)
        q_ref[:, sl] = _rope(qn[:, sl], cos, sin).astype(bf16)
    k_ms = _dot((k * k).astype(bf16), ones_ref[0:KV_WIDTH, 0:KV_WIDTH])
    kr = _rope(k * lax.rsqrt(k_ms + EPS) * kn_ref[...], cos, sin)
    lane = lax.broadcasted_iota(jnp.int32, kr.shape, 1)
    left = lane < HEAD_DIM
    ksw = pltpu.roll(kr, HEAD_DIM, 1)
    k2_ref[:, 0:LANES] = jnp.where(left, kr, ksw).astype(bf16)
    k2_ref[:, LANES:2 * LANES] = jnp.where(left, ksw, kr).astype(bf16)
    vsw = pltpu.roll(v, HEAD_DIM, 1)
    zero = jnp.zeros_like(v)
    v2_ref[:, 0:LANES] = jnp.where(left, v, zero).astype(bf16)
    v2_ref[:, LANES:2 * LANES] = jnp.where(left, zero, vsw).astype(bf16)
    v2_ref[:, 2 * LANES:3 * LANES] = jnp.where(left, vsw, zero).astype(bf16)
    v2_ref[:, 3 * LANES:4 * LANES] = jnp.where(left, zero, v).astype(bf16)


def _mod_row(b, j, n_batch):
    return jnp.where(j == 0, n_batch, b)


def _inproj(layer, xc, mod, g, w_in, cos_t, sin_t, qn, kn, ones):
    B, NT, D = xc.shape
    T = TOKEN_TILE
    nj = NT // T
    win = w_in.shape[2]
    tok = lambda w: pl.BlockSpec((None, T, w), lambda b, j: (b, j, 0))
    return pl.pallas_call(
        _inproj_kernel,
        out_shape=(
            jax.ShapeDtypeStruct((B, NT, POOL_WIDTH), f32),
            jax.ShapeDtypeStruct((B, NT, SSM_WIDTH), f32),
            jax.ShapeDtypeStruct((B, NT, ATTN_WIDTH), bf16),
            jax.ShapeDtypeStruct((B, NT, 2 * KV_WIDTH), bf16),
            jax.ShapeDtypeStruct((B, NT, 4 * KV_WIDTH), bf16),
        ),
        grid=(B, nj),
        in_specs=[
            tok(D),
            pl.BlockSpec((None, None, N_MOD, D), lambda b, j: (layer, _mod_row(b, j, B), 0, 0)),
            _layer_spec((1, D), layer),
            _layer_spec((D, win), layer),
            pl.BlockSpec((T, LANES), lambda b, j: (j, 0)),
            pl.BlockSpec((T, LANES), lambda b, j: (j, 0)),
            _layer_spec((1, ATTN_WIDTH), layer),
            _layer_spec((1, KV_WIDTH), layer),
            _const_spec((ATTN_WIDTH, ATTN_WIDTH)),
        ],
        out_specs=(tok(POOL_WIDTH), tok(SSM_WIDTH), tok(ATTN_WIDTH), tok(2 * KV_WIDTH),
                   tok(4 * KV_WIDTH)),
        compiler_params=_params("arbitrary", "arbitrary"),
        name="inproj",
    )(xc, mod, g, w_in, cos_t, sin_t, qn, kn, ones)


def _shift_rows(x, s, row):
    n = x.shape[0]
    if s > 0:
        return jnp.where(row >= s, pltpu.roll(x, s, 0), 0.0)
    return jnp.where(row < n + s, pltpu.roll(x, n + s, 0), 0.0)


def _window_sums(u, half_a, half_b, lane_split):
    row = lax.broadcasted_iota(jnp.int32, u.shape, 0)
    lane = lax.broadcasted_iota(jnp.int32, u.shape, 1)
    bwd = u
    fwd = u
    k = 1
    out_a = None
    while True:
        if k == half_a:
            out_a = _shift_rows(bwd, 1, row) + fwd
        if k == half_b:
            out_b = _shift_rows(bwd, 1, row) + fwd
            break
        bwd = bwd + _shift_rows(bwd, k, row)
        fwd = fwd + _shift_rows(fwd, -k, row)
        k *= 2
    n = u.shape[0]
    half = jnp.where(lane < lane_split, half_a, half_b)
    cnt = jnp.minimum(row + half, n) - jnp.maximum(row - half, 0)
    return jnp.where(lane < lane_split, out_a, out_b) / cnt.astype(f32)


def _pool_segment(u_ref, w_ref, scale_ref, o_ref, r0, n):
    parts = []
    for blk in range(POOL_WIDTH // LANES):
        u = u_ref[r0:r0 + n, blk * LANES:(blk + 1) * LANES]
        wa, wb = POOL_WINDOWS[2 * blk], POOL_WINDOWS[2 * blk + 1]
        mean = _window_sums(u, wa // 2, wb // 2, POOL_GROUP)
        parts.append((mean - u).astype(bf16))
    d = jnp.concatenate(parts, axis=1)
    o_ref[r0:r0 + n, :] = (_dot(d, w_ref[...]) * scale_ref[...]).astype(bf16)


def _pool_kernel(u_ref, w_ref, scale_ref, o_ref, *, n_ctx):
    nt = u_ref.shape[0]
    _pool_segment(u_ref, w_ref, scale_ref, o_ref, 0, n_ctx)
    _pool_segment(u_ref, w_ref, scale_ref, o_ref, n_ctx, nt - n_ctx)


def _pool(layer, pool_u, w_bd, scale, n_ctx):
    B, NT, W = pool_u.shape
    return pl.pallas_call(
        functools.partial(_pool_kernel, n_ctx=n_ctx),
        out_shape=jax.ShapeDtypeStruct((B, NT, W), bf16),
        grid=(B,),
        in_specs=[
            pl.BlockSpec((None, NT, W), lambda b: (b, 0, 0)),
            _layer_spec((W, W), layer),
            _layer_spec((1, W), layer),
        ],
        out_specs=pl.BlockSpec((None, NT, W), lambda b: (b, 0, 0)),
        compiler_params=_params("arbitrary"),
        name="pool",
    )(pool_u, w_bd, scale)


def _ssm_kernel(u_ref, bmat_ref, cmat_ref, are_ref, aim_ref, y_ref,
                upad_ref, bu0_ref, bu1_ref, s0_ref, s1_ref, state_ref, *, n_chunks):
    g = pl.program_id(0)
    nb = u_ref.shape[0]
    ts, pitch = SCAN_STEPS, SCAN_PITCH
    n_seq = 2 * n_chunks
    nk = SSM_LANES // LANES

    @pl.when(g == 0)
    def _():
        for r in (upad_ref, bu0_ref, bu1_ref, s0_ref, s1_ref, state_ref):
            r[...] = jnp.zeros_like(r)

    def direction(i):
        return jnp.clip(i, 0, n_seq - 1) // n_chunks

    def half_rows(i, h):
        off = jnp.where(direction(i) == 0, h, 1 - h) * ts
        return pl.ds(pl.multiple_of(off, ts), ts)

    def stage_b(i, h, bu_ref):
        rows = half_rows(i, h)
        for b in range(nb):
            upad_ref[b * pitch:b * pitch + ts, :] = u_ref[b, rows, :]
        res = _dot(upad_ref[...].astype(bf16), bmat_ref[direction(i)])
        for k in range(2 * nk):
            bu_ref[k] = res[:, k * LANES:(k + 1) * LANES]

    def stage_scan(i, bu_ref, s_ref):
        ic = jnp.clip(i, 0, n_seq - 1)
        d = ic // n_chunks
        keep = jnp.where((ic % n_chunks) == 0, 0.0, 1.0)
        a_re = are_ref[d]
        a_im = aim_ref[d]
        st_re = [state_ref[:, k * LANES:(k + 1) * LANES] * keep for k in range(nk)]
        st_im = [state_ref[:, (nk + k) * LANES:(nk + k + 1) * LANES] * keep for k in range(nk)]
        for t in range(ts):
            tt = jnp.where(d == 0, t, ts - 1 - t)
            rows = pl.ds(tt, nb, stride=pitch)
            for k in range(nk):
                ar = a_re[:, k * LANES:(k + 1) * LANES]
                ai = a_im[:, k * LANES:(k + 1) * LANES]
                n_re = ar * st_re[k] - ai * st_im[k] + bu_ref[k, rows, :]
                n_im = ar * st_im[k] + ai * st_re[k] + bu_ref[nk + k, rows, :]
                s_ref[k, rows, :] = n_re
                s_ref[nk + k, rows, :] = n_im
                st_re[k], st_im[k] = n_re, n_im
        for k in range(nk):
            state_ref[:, k * LANES:(k + 1) * LANES] = st_re[k]
            state_ref[:, (nk + k) * LANES:(nk + k + 1) * LANES] = st_im[k]

    def stage_c(i, h, s_ref):
        s = jnp.concatenate([s_ref[k] for k in range(2 * nk)], axis=1).astype(bf16)
        yv = _dot(s, cmat_ref[direction(i)])
        rows = half_rows(i, h)
        for b in range(nb):
            y_ref[b, rows, :] = yv[b * pitch:b * pitch + ts, :]

    stage_b(2 * g, 0, bu0_ref)
    stage_c(2 * g - 2, 0, s0_ref)
    stage_scan(2 * g - 1, bu1_ref, s1_ref)
    stage_b(2 * g + 1, 1, bu1_ref)
    stage_scan(2 * g, bu0_ref, s0_ref)
    stage_c(2 * g - 1, 1, s1_ref)


def _ssm(layer, u, bmat, cmat, a_re, a_im, n_ctx):
    B, NT, W = u.shape
    ts = SCAN_STEPS
    n_chunks = NT // ts
    n_blocks = n_chunks // 2
    nc_blocks = n_ctx // (2 * ts)

    def block(step):
        step = jnp.clip(step, 0, 2 * n_blocks - 1)
        d = step // n_blocks
        j = step % n_blocks
        back = jnp.where(j < nc_blocks, nc_blocks - 1 - j, n_blocks + nc_blocks - 1 - j)
        return d, jnp.where(d == 0, j, back)

    rows = B * SCAN_PITCH
    slab = pltpu.VMEM((2 * SSM_LANES // LANES, rows, LANES), f32)
    return pl.pallas_call(
        functools.partial(_ssm_kernel, n_chunks=n_chunks),
        out_shape=jax.ShapeDtypeStruct((2, B, NT, W), f32),
        grid=(2 * n_blocks + 1,),
        in_specs=[
            pl.BlockSpec((B, 2 * ts, W), lambda g: (0, block(g)[1], 0)),
            _layer_spec((2, W, 2 * SSM_LANES), layer),
            _layer_spec((2, 2 * SSM_LANES, W), layer),
            _layer_spec((2, B, SSM_LANES), layer),
            _layer_spec((2, B, SSM_LANES), layer),
        ],
        out_specs=pl.BlockSpec((None, B, 2 * ts, W),
                               lambda g: (block(g - 1)[0], 0, block(g - 1)[1], 0)),
        scratch_shapes=[pltpu.VMEM((rows, W), f32), slab, slab, slab, slab,
                        pltpu.VMEM((B, 2 * SSM_LANES), f32)],
        compiler_params=_params("arbitrary"),
        name="ssm_scan",
    )(u, bmat, cmat, a_re, a_im)


def _attend_pair(qp, keys, vals_l, vals_r, masks, sink_a, sink_b):
    nq = qp.shape[0]
    lane = lax.broadcasted_iota(jnp.int32, qp.shape, 1)
    zero = jnp.zeros_like(qp)
    lhs = jnp.concatenate([jnp.where(lane < HEAD_DIM, qp, zero),
                           jnp.where(lane < HEAD_DIM, zero, qp)], axis=0)
    row = lax.broadcasted_iota(jnp.int32, (2 * nq, 1), 0)
    sink = jnp.where(row < nq, sink_a, sink_b)
    scores = []
    m = sink
    for kk, mk in zip(keys, masks):
        s = _dot_nt(lhs, kk)
        if mk is not None:
            s = jnp.where(jnp.concatenate([mk, mk], axis=0), s, -jnp.inf)
        scores.append(s)
        m = jnp.maximum(m, jnp.max(s, axis=-1, keepdims=True))
    den = jnp.exp(sink - m)
    out_a = jnp.zeros((nq, LANES), f32)
    out_b = jnp.zeros((nq, LANES), f32)
    for s, vl, vr in zip(scores, vals_l, vals_r):
        p = jnp.exp(s - m)
        den = den + jnp.sum(p, axis=-1, keepdims=True)
        pb = p.astype(bf16)
        out_a = out_a + _dot(pb[0:nq], vl)
        out_b = out_b + _dot(pb[nq:2 * nq], vr)
    return out_a / den[0:nq] + out_b / den[nq:2 * nq]


def _attn_kernel(sink_ref, q_ref, k2_ref, v2_ref, o_ref, *, n_ctx):
    i = pl.program_id(1)
    nq = q_ref.shape[0]
    n_lat = k2_ref.shape[0] - n_ctx
    n_ctx_tiles = n_ctx // nq
    span = 3 * WINDOW

    def run(keys_rows, masks):
        for hk in range(N_KV_HEADS):
            keys = [k2_ref[r, hk * LANES:(hk + 1) * LANES] for r in keys_rows]
            vals_l = [v2_ref[r, 2 * hk * LANES:(2 * hk + 1) * LANES] for r in keys_rows]
            vals_r = [v2_ref[r, (2 * hk + 1) * LANES:(2 * hk + 2) * LANES] for r in keys_rows]
            for p in range(GROUP // 2):
                c0 = (hk * (GROUP // 2) + p) * LANES
                h_a = hk * GROUP + 2 * p
                out = _attend_pair(q_ref[:, c0:c0 + LANES], keys, vals_l, vals_r, masks,
                                   sink_ref[h_a], sink_ref[h_a + 1])
                o_ref[:, c0:c0 + LANES] = out.astype(bf16)

    ctx_rows = pl.ds(0, n_ctx)

    @pl.when(i < n_ctx_tiles)
    def _():
        run([ctx_rows], [None])

    @pl.when(i >= n_ctx_tiles)
    def _():
        q0 = (i - n_ctx_tiles) * nq
        start = jnp.clip(q0 - WINDOW, 0, n_lat - span)
        q_pos = q0 + lax.broadcasted_iota(jnp.int32, (nq, span), 0)
        k_pos = start + lax.broadcasted_iota(jnp.int32, (nq, span), 1)
        valid = jnp.abs(k_pos - q_pos) <= WINDOW
        win_rows = pl.ds(pl.multiple_of(n_ctx + start, WINDOW), span)
        run([win_rows, ctx_rows], [valid, None])


def _attention(layer, sink, q, k2, v2, n_ctx):
    B, NT, _ = q.shape
    return pl.pallas_call(
        functools.partial(_attn_kernel, n_ctx=n_ctx),
        out_shape=jax.ShapeDtypeStruct((B, NT, ATTN_WIDTH), bf16),
        grid=(B, NT // Q_TILE),
        in_specs=[
            pl.BlockSpec(memory_space=pltpu.SMEM),
            pl.BlockSpec((None, Q_TILE, ATTN_WIDTH), lambda b, i: (b, i, 0)),
            pl.BlockSpec((None, NT, 2 * KV_WIDTH), lambda b, i: (b, 0, 0)),
            pl.BlockSpec((None, NT, 4 * KV_WIDTH), lambda b, i: (b, 0, 0)),
        ],
        out_specs=pl.BlockSpec((None, Q_TILE, ATTN_WIDTH), lambda b, i: (b, i, 0)),
        compiler_params=_params("arbitrary", "arbitrary"),
        name="attention",
    )(sink[layer], q, k2, v2)


def _outffn_kernel(x_ref, mod_ref, po_ref, u_ref, y0_ref, y1_ref, ao_ref, d_ref, gw_ref, gb_ref,
                   wo_ref, g_ref, wg_ref, wu_ref, wd_ref, o_ref, *, ff_chunk):
    y = d_ref[...] * u_ref[...] + y0_ref[...] + y1_ref[...]
    ge = _gelu_tanh(y)
    so = ge * _sigmoid(_dot(ge.astype(bf16), gw_ref[...]) + gb_ref[...])
    o1 = POOL_WIDTH
    o2 = POOL_WIDTH + SSM_WIDTH
    mix = (_dot(po_ref[...], wo_ref[0:o1, :]) + _dot(so.astype(bf16), wo_ref[o1:o2, :])
           + _dot(ao_ref[...], wo_ref[o2:, :]))
    x1 = x_ref[...] + mod_ref[2:3, :] * mix
    h = _rmsnorm_mod(x1, g_ref[...], mod_ref[4:5, :], mod_ref[3:4, :]).astype(bf16)
    d_ff = wg_ref.shape[1]
    acc = jnp.zeros_like(x1)
    for c0 in range(0, d_ff, ff_chunk):
        gate = _dot(h, wg_ref[:, c0:c0 + ff_chunk])
        up = _dot(h, wu_ref[:, c0:c0 + ff_chunk])
        acc = acc + _dot((_silu(gate) * up).astype(bf16), wd_ref[c0:c0 + ff_chunk, :])
    o_ref[...] = x1 + mod_ref[5:6, :] * acc


def _outffn(layer, xc, mod, po, u, y, ao, d_skip, glu_w, glu_b, w_out, g, w_gate, w_up, w_down,
            skip_ctx):
    B, NT, D = xc.shape
    T = TOKEN_TILE
    j0 = 1 if skip_ctx else 0
    nj = NT // T - j0
    d_ff = w_gate.shape[2]
    tok = lambda w: pl.BlockSpec((None, T, w), lambda b, j: (b, j + j0, 0))
    ydir = lambda d: pl.BlockSpec((None, None, T, SSM_WIDTH), lambda b, j: (d, b, j + j0, 0))
    return pl.pallas_call(
        functools.partial(_outffn_kernel, ff_chunk=d_ff // 2),
        out_shape=jax.ShapeDtypeStruct((B, nj * T, D), f32),
        grid=(B, nj),
        in_specs=[
            tok(D),
            pl.BlockSpec((None, None, N_MOD, D),
                         lambda b, j: (layer, _mod_row(b, j + j0, B), 0, 0)),
            tok(POOL_WIDTH),
            tok(SSM_WIDTH),
            ydir(0),
            ydir(1),
            tok(ATTN_WIDTH),
            _layer_spec((1, SSM_WIDTH), layer),
            _layer_spec((SSM_WIDTH, SSM_WIDTH), layer),
            _layer_spec((1, SSM_WIDTH), layer),
            _layer_spec((D, D), layer),
            _layer_spec((1, D), layer),
            _layer_spec((D, d_ff), layer),
            _layer_spec((D, d_ff), layer),
            _layer_spec((d_ff, D), layer),
        ],
        out_specs=pl.BlockSpec((None, T, D), lambda b, j: (b, j, 0)),
        compiler_params=_params("arbitrary", "arbitrary"),
        name="outproj_ffn",
    )(xc, mod, po, u, y, y, ao, d_skip, glu_w, glu_b, w_out, g, w_gate, w_up, w_down)


def _rope_tables(n_ctx, n_lat):
    pos = jnp.arange(n_lat)
    row = (pos // GRID_W).astype(f32)
    col = (pos % GRID_W).astype(f32)
    inv = jnp.power(ROPE_BASE, -jnp.arange(ROPE_FREQS, dtype=f32) / ROPE_FREQS)
    lane = jnp.arange(LANES)
    dim = lane % HEAD_DIM
    freq = inv[dim % ROPE_FREQS]
    ang = jnp.where((dim // (2 * ROPE_FREQS)) == 0, row[:, None], col[:, None]) * freq[None, :]
    sign = jnp.where((dim % (2 * ROPE_FREQS)) < ROPE_FREQS, -1.0, 1.0).astype(f32)
    cos = jnp.concatenate([jnp.ones((n_ctx, LANES), f32), jnp.cos(ang)], axis=0)
    sin = jnp.concatenate([jnp.zeros((n_ctx, LANES), f32), jnp.sin(ang) * sign], axis=0)
    return cos, sin


def _block_diag(blocks):
    *lead, g, r, c = blocks.shape
    eye = jnp.eye(g, dtype=blocks.dtype)
    return jnp.einsum('...grc,gh->...grhc', blocks, eye).reshape(*lead, g * r, g * c)


def _ssm_matrices(a_re, a_im, log_dt, b_re, b_im, c_re, c_im, n_batch):
    lr = jnp.minimum(a_re.astype(f32), -1e-4)
    li = a_im.astype(f32)
    dt = jnp.exp(log_dt.astype(f32))[..., None]
    mag = jnp.exp(lr * dt)
    ar = mag * jnp.cos(li * dt)
    ai = mag * jnp.sin(li * dt)
    den = lr * lr + li * li
    qr = ((ar - 1.0) * lr + ai * li) / den
    qi = (ai * lr - (ar - 1.0) * li) / den
    bbr = qr[..., None] * b_re - qi[..., None] * b_im
    bbi = qr[..., None] * b_im + qi[..., None] * b_re
    bmat = jnp.concatenate([_block_diag(jnp.swapaxes(bbr, -1, -2)),
                            _block_diag(jnp.swapaxes(bbi, -1, -2))], axis=-1)
    cmat = jnp.concatenate([_block_diag(jnp.swapaxes(c_re.astype(f32), -1, -2)),
                            -_block_diag(jnp.swapaxes(c_im.astype(f32), -1, -2))], axis=-2)
    shape = a_re.shape[:2] + (n_batch, SSM_LANES)
    a_re_b = jnp.broadcast_to(ar.reshape(a_re.shape[:2] + (1, SSM_LANES)), shape)
    a_im_b = jnp.broadcast_to(ai.reshape(a_re.shape[:2] + (1, SSM_LANES)), shape)
    return bmat.astype(bf16), cmat.astype(bf16), a_re_b, a_im_b


def kernel(x, c, ctx, c_ctx, w_mod, b_mod, norm_mix, norm_ffn, w_in, w_out, pool_w, pool_scale, ssm_a_re, ssm_a_im, ssm_log_dt, ssm_b_re, ssm_b_im, ssm_c_re, ssm_c_im, ssm_d, ssm_glu_w, ssm_glu_b, q_norm, k_norm, attn_sink, ffn_w_gate, ffn_w_up, ffn_w_down):
    B, L, D = x.shape
    Lc = ctx.shape[1]
    depth = w_mod.shape[0]
    assert Lc == TOKEN_TILE and L % TOKEN_TILE == 0 and L % GRID_W == 0 and L >= 3 * WINDOW
    assert B == SUBLANES and w_in.shape[2] == POOL_WIDTH + SSM_WIDTH + ATTN_WIDTH + 2 * KV_WIDTH
    assert w_mod.shape[2] == N_MOD * D and Lc % (4 * SCAN_STEPS) == 0 and L % (4 * SCAN_STEPS) == 0

    c16 = jnp.concatenate([c, c_ctx[None, :], jnp.zeros((16 - B - 1, D), f32)], axis=0)
    mod = _modulation(c16, w_mod, b_mod).reshape(depth, 16, N_MOD, D)

    cos_t, sin_t = _rope_tables(Lc, L)
    head = jnp.arange(ATTN_WIDTH) // HEAD_DIM
    ones = jnp.where(head[:, None] == head[None, :], 1.0 / HEAD_DIM, 0.0).astype(bf16)
    qn = (jnp.tile(q_norm, (1, N_HEADS)) * HEAD_DIM ** -0.5).reshape(depth, 1, ATTN_WIDTH)
    kn = jnp.tile(k_norm, (1, N_KV_HEADS)).reshape(depth, 1, KV_WIDTH)
    bmat, cmat, a_re, a_im = _ssm_matrices(ssm_a_re, ssm_a_im, ssm_log_dt, ssm_b_re, ssm_b_im,
                                           ssm_c_re, ssm_c_im, B)
    pool_bd = _block_diag(pool_w).astype(bf16)
    w_in_b, w_out_b, glu_w_b = w_in.astype(bf16), w_out.astype(bf16), ssm_glu_w.astype(bf16)
    w_gate_b, w_up_b, w_down_b = ffn_w_gate.astype(bf16), ffn_w_up.astype(bf16), ffn_w_down.astype(bf16)
    row = lambda a: a.reshape(depth, 1, a.shape[-1])

    xc = jnp.concatenate([ctx, x], axis=1)
    for l in range(depth):
        pool_u, u, q, k2, v2 = _inproj(l, xc, mod, row(norm_mix), w_in_b, cos_t, sin_t, qn, kn, ones)
        po = _pool(l, pool_u, pool_bd, row(pool_scale), Lc)
        y = _ssm(l, u, bmat, cmat, a_re, a_im, Lc)
        ao = _attention(l, attn_sink, q, k2, v2, Lc)
        xc = _outffn(l, xc, mod, po, u, y, ao, row(ssm_d), glu_w_b, row(ssm_glu_b), w_out_b,
                     row(norm_ffn), w_gate_b, w_up_b, w_down_b, skip_ctx=(l == depth - 1))
    return xc
```

```python
import functools
import math

import jax
import jax.numpy as jnp
from jax import lax
from jax.experimental import pallas as pl
from jax.experimental.pallas import tpu as pltpu

f32 = jnp.float32
bf16 = jnp.bfloat16

GRID_W = 64
POOL_WINDOWS = (2, 4, 8, 16)
POOL_GROUP = 64
POOL_WIDTH = POOL_GROUP * len(POOL_WINDOWS)
SSM_GROUP_CH = 16
SSM_GROUPS = 16
SSM_WIDTH = SSM_GROUP_CH * SSM_GROUPS
SSM_STATE = 64
SSM_LANES = SSM_GROUPS * SSM_STATE
HEAD_DIM = 64
N_HEADS = 8
N_KV_HEADS = 2
GROUP = N_HEADS // N_KV_HEADS
ATTN_WIDTH = N_HEADS * HEAD_DIM
KV_WIDTH = N_KV_HEADS * HEAD_DIM
WINDOW = 128
ROPE_BASE = 10000.0
ROPE_FREQS = HEAD_DIM // 4
N_MOD = 6
EPS = 1e-6

LANES = 128
SUBLANES = 8
TOKEN_TILE = 256
Q_TILE = 128
SCAN_STEPS = 64
SCAN_PITCH = SCAN_STEPS + 4
MXU_TILE = 256
N_SUB = 2
FF_CHUNK = 6 * MXU_TILE
VMEM_LIMIT_BYTES = 56 * 1024 * 1024


def _sigmoid(x):
    return 1.0 / (1.0 + jnp.exp(-x))


def _silu(x):
    return x * _sigmoid(x)


def _gelu_tanh(x):
    return 0.5 * x * (1.0 + jnp.tanh(math.sqrt(2.0 / math.pi) * (x + 0.044715 * (x * x * x))))


def _dot(a, b):
    return jnp.dot(a, b, preferred_element_type=f32)


def _dot_nt(a, b):
    return lax.dot_general(a, b, (((1,), (1,)), ((), ())), preferred_element_type=f32)


def _const_spec(shape):
    nd = len(shape)
    return pl.BlockSpec(shape, lambda *_: (0,) * nd, pipeline_mode=pl.Buffered(1))


def _layer_spec(shape, layer):
    nd = len(shape)
    return pl.BlockSpec((None,) + tuple(shape), lambda *_: (layer,) + (0,) * nd,
                        pipeline_mode=pl.Buffered(1))


def _params(*sem):
    return pltpu.CompilerParams(dimension_semantics=sem, vmem_limit_bytes=VMEM_LIMIT_BYTES)


def _mod_kernel(c_ref, w_ref, b_ref, o_ref):
    a = _silu(c_ref[...]).astype(bf16)
    o_ref[...] = _dot(a, w_ref[...].astype(bf16)) + b_ref[...]


def _modulation(c16, w_mod, b_mod):
    depth, d, n = w_mod.shape
    tn = 1536
    return pl.pallas_call(
        _mod_kernel,
        out_shape=jax.ShapeDtypeStruct((depth, 16, n), f32),
        grid=(depth, n // tn),
        in_specs=[
            pl.BlockSpec((16, d), lambda l, j: (0, 0)),
            pl.BlockSpec((None, d, tn), lambda l, j: (l, 0, j)),
            pl.BlockSpec((None, 1, tn), lambda l, j: (l, 0, j)),
        ],
        out_specs=pl.BlockSpec((None, 16, tn), lambda l, j: (l, 0, j)),
        compiler_params=_params("arbitrary", "arbitrary"),
        name="modulation",
    )(c16, w_mod, b_mod.reshape(depth, 1, n))


def _rmsnorm_mod(x, g, scale, shift):
    y = x * lax.rsqrt(jnp.mean(x * x, axis=-1, keepdims=True) + EPS) * g
    return y * (1.0 + scale) + shift


def _rope(x, cos, sin_signed):
    lane = lax.broadcasted_iota(jnp.int32, x.shape, 1)
    first = (lane % 32) < 16
    partner = jnp.where(first, pltpu.roll(x, LANES - 16, 1), pltpu.roll(x, 16, 1))
    return x * cos + partner * sin_signed


def _inproj_kernel(x_ref, mod_ref, g_ref, w_ref, cos_ref, sin_ref, qn_ref, kn_ref, ones_ref,
                   pool_ref, ssm_ref, q_ref, k2_ref, v2_ref):
    h = _rmsnorm_mod(x_ref[...], g_ref[...], mod_ref[1:2, :], mod_ref[0:1, :]).astype(bf16)
    proj = _dot(h, w_ref[...])
    pool_ref[...] = proj[:, 0:POOL_WIDTH]
    ssm_ref[...] = proj[:, POOL_WIDTH:POOL_WIDTH + SSM_WIDTH]
    o = POOL_WIDTH + SSM_WIDTH
    q = proj[:, o:o + ATTN_WIDTH]
    k = proj[:, o + ATTN_WIDTH:o + ATTN_WIDTH + KV_WIDTH]
    v = proj[:, o + ATTN_WIDTH + KV_WIDTH:o + ATTN_WIDTH + 2 * KV_WIDTH]
    cos = cos_ref[...]
    sin = sin_ref[...]
    q_ms = _dot((q * q).astype(bf16), ones_ref[...])
    qn = q * lax.rsqrt(q_ms + EPS) * qn_ref[...]
    for cblk in range(ATTN_WIDTH // LANES):
        sl = slice(cblk * LANES, (cblk + 1) * LANES)
        q_ref[:, sl] = _rope(qn[:, sl], cos, sin).astype(bf16)
    k_ms = _dot((k * k).astype(bf16), ones_ref[0:KV_WIDTH, 0:KV_WIDTH])
    kr = _rope(k * lax.rsqrt(k_ms + EPS) * kn_ref[...], cos, sin)
    lane = lax.broadcasted_iota(jnp.int32, kr.shape, 1)
    left = lane < HEAD_DIM
    ksw = pltpu.roll(kr, HEAD_DIM, 1)
    k2_ref[:, 0:LANES] = jnp.where(left, kr, ksw).astype(bf16)
    k2_ref[:, LANES:2 * LANES] = jnp.where(left, ksw, kr).astype(bf16)
    vsw = pltpu.roll(v, HEAD_DIM, 1)
    zero = jnp.zeros_like(v)
    v2_ref[:, 0:LANES] = jnp.where(left, v, zero).astype(bf16)
    v2_ref[:, LANES:2 * LANES] = jnp.where(left, zero, vsw).astype(bf16)
    v2_ref[:, 2 * LANES:3 * LANES] = jnp.where(left, vsw, zero).astype(bf16)
    v2_ref[:, 3 * LANES:4 * LANES] = jnp.where(left, zero, v).astype(bf16)


def _mod_row(b, j, n_batch):
    return jnp.where(j == 0, n_batch, b)


def _inproj(layer, xc, mod, g, w_in, cos_t, sin_t, qn, kn, ones):
    B, NT, D = xc.shape
    T = TOKEN_TILE
    nj = NT // T
    win = w_in.shape[2]
    tok = lambda w: pl.BlockSpec((None, T, w), lambda b, j: (b, j, 0))
    return pl.pallas_call(
        _inproj_kernel,
        out_shape=(
            jax.ShapeDtypeStruct((B, NT, POOL_WIDTH), f32),
            jax.ShapeDtypeStruct((B, NT, SSM_WIDTH), f32),
            jax.ShapeDtypeStruct((B, NT, ATTN_WIDTH), bf16),
            jax.ShapeDtypeStruct((B, NT, 2 * KV_WIDTH), bf16),
            jax.ShapeDtypeStruct((B, NT, 4 * KV_WIDTH), bf16),
        ),
        grid=(B, nj),
        in_specs=[
            tok(D),
            pl.BlockSpec((None, None, N_MOD, D), lambda b, j: (layer, _mod_row(b, j, B), 0, 0)),
            _layer_spec((1, D), layer),
            _layer_spec((D, win), layer),
            pl.BlockSpec((T, LANES), lambda b, j: (j, 0)),
            pl.BlockSpec((T, LANES), lambda b, j: (j, 0)),
            _layer_spec((1, ATTN_WIDTH), layer),
            _layer_spec((1, KV_WIDTH), layer),
            _const_spec((ATTN_WIDTH, ATTN_WIDTH)),
        ],
        out_specs=(tok(POOL_WIDTH), tok(SSM_WIDTH), tok(ATTN_WIDTH), tok(2 * KV_WIDTH),
                   tok(4 * KV_WIDTH)),
        compiler_params=_params("arbitrary", "arbitrary"),
        name="inproj",
    )(xc, mod, g, w_in, cos_t, sin_t, qn, kn, ones)


def _shift_rows(x, s, row):
    n = x.shape[0]
    if s > 0:
        return jnp.where(row >= s, pltpu.roll(x, s, 0), 0.0)
    return jnp.where(row < n + s, pltpu.roll(x, n + s, 0), 0.0)


def _window_sums(u, half_a, half_b, lane_split):
    row = lax.broadcasted_iota(jnp.int32, u.shape, 0)
    lane = lax.broadcasted_iota(jnp.int32, u.shape, 1)
    bwd = u
    fwd = u
    k = 1
    out_a = None
    while True:
        if k == half_a:
            out_a = _shift_rows(bwd, 1, row) + fwd
        if k == half_b:
            out_b = _shift_rows(bwd, 1, row) + fwd
            break
        bwd = bwd + _shift_rows(bwd, k, row)
        fwd = fwd + _shift_rows(fwd, -k, row)
        k *= 2
    n = u.shape[0]
    half = jnp.where(lane < lane_split, half_a, half_b)
    cnt = jnp.minimum(row + half, n) - jnp.maximum(row - half, 0)
    return jnp.where(lane < lane_split, out_a, out_b) / cnt.astype(f32)


def _pool_segment(u_ref, w_ref, scale_ref, o_ref, r0, n):
    parts = []
    for blk in range(POOL_WIDTH // LANES):
        u = u_ref[r0:r0 + n, blk * LANES:(blk + 1) * LANES]
        wa, wb = POOL_WINDOWS[2 * blk], POOL_WINDOWS[2 * blk + 1]
        mean = _window_sums(u, wa // 2, wb // 2, POOL_GROUP)
        parts.append((mean - u).astype(bf16))
    d = jnp.concatenate(parts, axis=1)
    o_ref[r0:r0 + n, :] = (_dot(d, w_ref[...]) * scale_ref[...]).astype(bf16)


def _pool_kernel(u_ref, w_ref, scale_ref, o_ref, *, n_ctx):
    nt = u_ref.shape[0]
    _pool_segment(u_ref, w_ref, scale_ref, o_ref, 0, n_ctx)
    _pool_segment(u_ref, w_ref, scale_ref, o_ref, n_ctx, nt - n_ctx)


def _pool(layer, pool_u, w_bd, scale, n_ctx):
    B, NT, W = pool_u.shape
    return pl.pallas_call(
        functools.partial(_pool_kernel, n_ctx=n_ctx),
        out_shape=jax.ShapeDtypeStruct((B, NT, W), bf16),
        grid=(B,),
        in_specs=[
            pl.BlockSpec((None, NT, W), lambda b: (b, 0, 0)),
            _layer_spec((W, W), layer),
            _layer_spec((1, W), layer),
        ],
        out_specs=pl.BlockSpec((None, NT, W), lambda b: (b, 0, 0)),
        compiler_params=_params("arbitrary"),
        name="pool",
    )(pool_u, w_bd, scale)


def _ssm_kernel(u_ref, bmat_ref, cmat_ref, are_ref, aim_ref, y_ref,
                upad_ref, bu0_ref, bu1_ref, s0_ref, s1_ref, state_ref, *, n_chunks):
    g = pl.program_id(0)
    nb = u_ref.shape[0]
    ts, pitch = SCAN_STEPS, SCAN_PITCH
    n_seq = 2 * n_chunks
    nk = SSM_LANES // LANES

    @pl.when(g == 0)
    def _():
        for r in (upad_ref, bu0_ref, bu1_ref, s0_ref, s1_ref, state_ref):
            r[...] = jnp.zeros_like(r)

    def direction(i):
        return jnp.clip(i, 0, n_seq - 1) // n_chunks

    def half_rows(i, h):
        off = jnp.where(direction(i) == 0, h, 1 - h) * ts
        return pl.ds(pl.multiple_of(off, ts), ts)

    def stage_b(i, h, bu_ref):
        rows = half_rows(i, h)
        for b in range(nb):
            upad_ref[b * pitch:b * pitch + ts, :] = u_ref[b, rows, :]
        res = _dot(upad_ref[...].astype(bf16), bmat_ref[direction(i)])
        for k in range(2 * nk):
            bu_ref[k] = res[:, k * LANES:(k + 1) * LANES]

    def stage_scan(i, bu_ref, s_ref):
        ic = jnp.clip(i, 0, n_seq - 1)
        d = ic // n_chunks
        keep = jnp.where((ic % n_chunks) == 0, 0.0, 1.0)
        a_re = are_ref[d]
        a_im = aim_ref[d]
        st_re = [state_ref[:, k * LANES:(k + 1) * LANES] * keep for k in range(nk)]
        st_im = [state_ref[:, (nk + k) * LANES:(nk + k + 1) * LANES] * keep for k in range(nk)]
        for t in range(ts):
            tt = jnp.where(d == 0, t, ts - 1 - t)
            rows = pl.ds(tt, nb, stride=pitch)
            for k in range(nk):
                ar = a_re[:, k * LANES:(k + 1) * LANES]
                ai = a_im[:, k * LANES:(k + 1) * LANES]
                n_re = ar * st_re[k] - ai * st_im[k] + bu_ref[k, rows, :]
                n_im = ar * st_im[k] + ai * st_re[k] + bu_ref[nk + k, rows, :]
                s_ref[k, rows, :] = n_re
                s_ref[nk + k, rows, :] = n_im
                st_re[k], st_im[k] = n_re, n_im
        for k in range(nk):
            state_ref[:, k * LANES:(k + 1) * LANES] = st_re[k]
            state_ref[:, (nk + k) * LANES:(nk + k + 1) * LANES] = st_im[k]

    def stage_c(i, h, s_ref):
        s = jnp.concatenate([s_ref[k] for k in range(2 * nk)], axis=1).astype(bf16)
        yv = _dot(s, cmat_ref[direction(i)])
        rows = half_rows(i, h)
        for b in range(nb):
            y_ref[b, rows, :] = yv[b * pitch:b * pitch + ts, :]

    stage_b(2 * g, 0, bu0_ref)
    stage_c(2 * g - 2, 0, s0_ref)
    stage_scan(2 * g - 1, bu1_ref, s1_ref)
    stage_b(2 * g + 1, 1, bu1_ref)
    stage_scan(2 * g, bu0_ref, s0_ref)
    stage_c(2 * g - 1, 1, s1_ref)


def _ssm(layer, u, bmat, cmat, a_re, a_im, n_ctx):
    B, NT, W = u.shape
    ts = SCAN_STEPS
    n_chunks = NT // ts
    n_blocks = n_chunks // 2
    nc_blocks = n_ctx // (2 * ts)

    def block(step):
        step = jnp.clip(step, 0, 2 * n_blocks - 1)
        d = step // n_blocks
        j = step % n_blocks
        back = jnp.where(j < nc_blocks, nc_blocks - 1 - j, n_blocks + nc_blocks - 1 - j)
        return d, jnp.where(d == 0, j, back)

    rows = B * SCAN_PITCH
    slab = pltpu.VMEM((2 * SSM_LANES // LANES, rows, LANES), f32)
    return pl.pallas_call(
        functools.partial(_ssm_kernel, n_chunks=n_chunks),
        out_shape=jax.ShapeDtypeStruct((2, B, NT, W), f32),
        grid=(2 * n_blocks + 1,),
        in_specs=[
            pl.BlockSpec((B, 2 * ts, W), lambda g: (0, block(g)[1], 0)),
            _layer_spec((2, W, 2 * SSM_LANES), layer),
            _layer_spec((2, 2 * SSM_LANES, W), layer),
            _layer_spec((2, B, SSM_LANES), layer),
            _layer_spec((2, B, SSM_LANES), layer),
        ],
        out_specs=pl.BlockSpec((None, B, 2 * ts, W),
                               lambda g: (block(g - 1)[0], 0, block(g - 1)[1], 0)),
        scratch_shapes=[pltpu.VMEM((rows, W), f32), slab, slab, slab, slab,
                        pltpu.VMEM((B, 2 * SSM_LANES), f32)],
        compiler_params=_params("arbitrary"),
        name="ssm_scan",
    )(u, bmat, cmat, a_re, a_im)


def _attend_pair(qp, keys, vals_l, vals_r, masks, sink_a, sink_b):
    nq = qp.shape[0]
    lane = lax.broadcasted_iota(jnp.int32, qp.shape, 1)
    zero = jnp.zeros_like(qp)
    lhs = jnp.concatenate([jnp.where(lane < HEAD_DIM, qp, zero),
                           jnp.where(lane < HEAD_DIM, zero, qp)], axis=0)
    row = lax.broadcasted_iota(jnp.int32, (2 * nq, 1), 0)
    sink = jnp.where(row < nq, sink_a, sink_b)
    scores = []
    m = sink
    for kk, mk in zip(keys, masks):
        s = _dot_nt(lhs, kk)
        if mk is not None:
            s = jnp.where(jnp.concatenate([mk, mk], axis=0), s, -jnp.inf)
        scores.append(s)
        m = jnp.maximum(m, jnp.max(s, axis=-1, keepdims=True))
    den = jnp.exp(sink - m)
    out_a = jnp.zeros((nq, LANES), f32)
    out_b = jnp.zeros((nq, LANES), f32)
    for s, vl, vr in zip(scores, vals_l, vals_r):
        p = jnp.exp(s - m)
        den = den + jnp.sum(p, axis=-1, keepdims=True)
        pb = p.astype(bf16)
        out_a = out_a + _dot(pb[0:nq], vl)
        out_b = out_b + _dot(pb[nq:2 * nq], vr)
    return out_a / den[0:nq] + out_b / den[nq:2 * nq]


def _attn_kernel(sink_ref, q_ref, k2_ref, v2_ref, o_ref, *, n_ctx):
    i = pl.program_id(1)
    nq = q_ref.shape[0]
    n_lat = k2_ref.shape[0] - n_ctx
    n_ctx_tiles = n_ctx // nq
    span = 3 * WINDOW

    def run(keys_rows, masks):
        for hk in range(N_KV_HEADS):
            keys = [k2_ref[r, hk * LANES:(hk + 1) * LANES] for r in keys_rows]
            vals_l = [v2_ref[r, 2 * hk * LANES:(2 * hk + 1) * LANES] for r in keys_rows]
            vals_r = [v2_ref[r, (2 * hk + 1) * LANES:(2 * hk + 2) * LANES] for r in keys_rows]
            for p in range(GROUP // 2):
                c0 = (hk * (GROUP // 2) + p) * LANES
                h_a = hk * GROUP + 2 * p
                out = _attend_pair(q_ref[:, c0:c0 + LANES], keys, vals_l, vals_r, masks,
                                   sink_ref[h_a], sink_ref[h_a + 1])
                o_ref[:, c0:c0 + LANES] = out.astype(bf16)

    ctx_rows = pl.ds(0, n_ctx)

    @pl.when(i < n_ctx_tiles)
    def _():
        run([ctx_rows], [None])

    @pl.when(i >= n_ctx_tiles)
    def _():
        q0 = (i - n_ctx_tiles) * nq
        start = jnp.clip(q0 - WINDOW, 0, n_lat - span)
        q_pos = q0 + lax.broadcasted_iota(jnp.int32, (nq, span), 0)
        k_pos = start + lax.broadcasted_iota(jnp.int32, (nq, span), 1)
        valid = jnp.abs(k_pos - q_pos) <= WINDOW
        win_rows = pl.ds(pl.multiple_of(n_ctx + start, WINDOW), span)
        run([win_rows, ctx_rows], [valid, None])


def _attention(layer, sink, q, k2, v2, n_ctx):
    B, NT, _ = q.shape
    return pl.pallas_call(
        functools.partial(_attn_kernel, n_ctx=n_ctx),
        out_shape=jax.ShapeDtypeStruct((B, NT, ATTN_WIDTH), bf16),
        grid=(B, NT // Q_TILE),
        in_specs=[
            pl.BlockSpec(memory_space=pltpu.SMEM),
            pl.BlockSpec((None, Q_TILE, ATTN_WIDTH), lambda b, i: (b, i, 0)),
            pl.BlockSpec((None, NT, 2 * KV_WIDTH), lambda b, i: (b, 0, 0)),
            pl.BlockSpec((None, NT, 4 * KV_WIDTH), lambda b, i: (b, 0, 0)),
        ],
        out_specs=pl.BlockSpec((None, Q_TILE, ATTN_WIDTH), lambda b, i: (b, i, 0)),
        compiler_params=_params("arbitrary", "arbitrary"),
        name="attention",
    )(sink[layer], q, k2, v2)


def _outffn_kernel(x_ref, mod_ref, po_ref, u_ref, y0_ref, y1_ref, ao_ref, d_ref, gw_ref, gb_ref,
                   wo_ref, g_ref, wg_ref, wu_ref, wd_ref, o_ref, *, ff_chunk):
    sub = x_ref.shape[0] // N_SUB
    o1 = POOL_WIDTH
    o2 = POOL_WIDTH + SSM_WIDTH
    d_ff = wg_ref.shape[1]

    def mixer(r):
        y = d_ref[...] * u_ref[r, :] + y0_ref[r, :] + y1_ref[r, :]
        ge = _gelu_tanh(y)
        so = ge * _sigmoid(_dot(ge.astype(bf16), gw_ref[...]) + gb_ref[...])
        mix = (_dot(po_ref[r, :], wo_ref[0:o1, :]) + _dot(so.astype(bf16), wo_ref[o1:o2, :])
               + _dot(ao_ref[r, :], wo_ref[o2:, :]))
        x1 = x_ref[r, :] + mod_ref[2:3, :] * mix
        return x1, _rmsnorm_mod(x1, g_ref[...], mod_ref[4:5, :], mod_ref[3:4, :]).astype(bf16)

    def ffn(r, x1, h):
        acc = jnp.zeros_like(x1)
        for c0 in range(0, d_ff, ff_chunk):
            c1 = min(c0 + ff_chunk, d_ff)
            gate = _dot(h, wg_ref[:, c0:c1])
            up = _dot(h, wu_ref[:, c0:c1])
            acc = acc + _dot((_silu(gate) * up).astype(bf16), wd_ref[c0:c1, :])
        o_ref[r, :] = x1 + mod_ref[5:6, :] * acc

    rows = [slice(si * sub, (si + 1) * sub) for si in range(N_SUB)]
    nxt = mixer(rows[0])
    for si in range(N_SUB):
        cur = nxt
        if si + 1 < N_SUB:
            nxt = mixer(rows[si + 1])
        ffn(rows[si], *cur)


def _outffn(layer, xc, mod, po, u, y, ao, d_skip, glu_w, glu_b, w_out, g, w_gate, w_up, w_down,
            skip_ctx):
    B, NT, D = xc.shape
    T = TOKEN_TILE
    j0 = 1 if skip_ctx else 0
    nj = NT // T - j0
    d_ff = w_gate.shape[2]
    tok = lambda w: pl.BlockSpec((None, T, w), lambda b, j: (b, j + j0, 0))
    ydir = lambda d: pl.BlockSpec((None, None, T, SSM_WIDTH), lambda b, j: (d, b, j + j0, 0))
    return pl.pallas_call(
        functools.partial(_outffn_kernel, ff_chunk=FF_CHUNK),
        out_shape=jax.ShapeDtypeStruct((B, nj * T, D), f32),
        grid=(B, nj),
        in_specs=[
            tok(D),
            pl.BlockSpec((None, None, N_MOD, D),
                         lambda b, j: (layer, _mod_row(b, j + j0, B), 0, 0)),
            tok(POOL_WIDTH),
            tok(SSM_WIDTH),
            ydir(0),
            ydir(1),
            tok(ATTN_WIDTH),
            _layer_spec((1, SSM_WIDTH), layer),
            _layer_spec((SSM_WIDTH, SSM_WIDTH), layer),
            _layer_spec((1, SSM_WIDTH), layer),
            _layer_spec((D, D), layer),
            _layer_spec((1, D), layer),
            _layer_spec((D, d_ff), layer),
            _layer_spec((D, d_ff), layer),
            _layer_spec((d_ff, D), layer),
        ],
        out_specs=pl.BlockSpec((None, T, D), lambda b, j: (b, j, 0)),
        compiler_params=_params("arbitrary", "arbitrary"),
        name="outproj_ffn",
    )(xc, mod, po, u, y, y, ao, d_skip, glu_w, glu_b, w_out, g, w_gate, w_up, w_down)


def _rope_tables(n_ctx, n_lat):
    pos = jnp.arange(n_lat)
    row = (pos // GRID_W).astype(f32)
    col = (pos % GRID_W).astype(f32)
    inv = jnp.power(ROPE_BASE, -jnp.arange(ROPE_FREQS, dtype=f32) / ROPE_FREQS)
    lane = jnp.arange(LANES)
    dim = lane % HEAD_DIM
    freq = inv[dim % ROPE_FREQS]
    ang = jnp.where((dim // (2 * ROPE_FREQS)) == 0, row[:, None], col[:, None]) * freq[None, :]
    sign = jnp.where((dim % (2 * ROPE_FREQS)) < ROPE_FREQS, -1.0, 1.0).astype(f32)
    cos = jnp.concatenate([jnp.ones((n_ctx, LANES), f32), jnp.cos(ang)], axis=0)
    sin = jnp.concatenate([jnp.zeros((n_ctx, LANES), f32), jnp.sin(ang) * sign], axis=0)
    return cos, sin


def _block_diag(blocks):
    *lead, g, r, c = blocks.shape
    eye = jnp.eye(g, dtype=blocks.dtype)
    return jnp.einsum('...grc,gh->...grhc', blocks, eye).reshape(*lead, g * r, g * c)


def _ssm_matrices(a_re, a_im, log_dt, b_re, b_im, c_re, c_im, n_batch):
    lr = jnp.minimum(a_re.astype(f32), -1e-4)
    li = a_im.astype(f32)
    dt = jnp.exp(log_dt.astype(f32))[..., None]
    mag = jnp.exp(lr * dt)
    ar = mag * jnp.cos(li * dt)
    ai = mag * jnp.sin(li * dt)
    den = lr * lr + li * li
    qr = ((ar - 1.0) * lr + ai * li) / den
    qi = (ai * lr - (ar - 1.0) * li) / den
    bbr = qr[..., None] * b_re - qi[..., None] * b_im
    bbi = qr[..., None] * b_im + qi[..., None] * b_re
    bmat = jnp.concatenate([_block_diag(jnp.swapaxes(bbr, -1, -2)),
                            _block_diag(jnp.swapaxes(bbi, -1, -2))], axis=-1)
    cmat = jnp.concatenate([_block_diag(jnp.swapaxes(c_re.astype(f32), -1, -2)),
                            -_block_diag(jnp.swapaxes(c_im.astype(f32), -1, -2))], axis=-2)
    shape = a_re.shape[:2] + (n_batch, SSM_LANES)
    a_re_b = jnp.broadcast_to(ar.reshape(a_re.shape[:2] + (1, SSM_LANES)), shape)
    a_im_b = jnp.broadcast_to(ai.reshape(a_re.shape[:2] + (1, SSM_LANES)), shape)
    return bmat.astype(bf16), cmat.astype(bf16), a_re_b, a_im_b


def kernel(x, c, ctx, c_ctx, w_mod, b_mod, norm_mix, norm_ffn, w_in, w_out, pool_w, pool_scale, ssm_a_re, ssm_a_im, ssm_log_dt, ssm_b_re, ssm_b_im, ssm_c_re, ssm_c_im, ssm_d, ssm_glu_w, ssm_glu_b, q_norm, k_norm, attn_sink, ffn_w_gate, ffn_w_up, ffn_w_down):
    B, L, D = x.shape
    Lc = ctx.shape[1]
    depth = w_mod.shape[0]
    assert Lc == TOKEN_TILE and L % TOKEN_TILE == 0 and L % GRID_W == 0 and L >= 3 * WINDOW
    assert B == SUBLANES and w_in.shape[2] == POOL_WIDTH + SSM_WIDTH + ATTN_WIDTH + 2 * KV_WIDTH
    assert w_mod.shape[2] == N_MOD * D and Lc % (4 * SCAN_STEPS) == 0 and L % (4 * SCAN_STEPS) == 0

    c16 = jnp.concatenate([c, c_ctx[None, :], jnp.zeros((16 - B - 1, D), f32)], axis=0)
    mod = _modulation(c16, w_mod, b_mod).reshape(depth, 16, N_MOD, D)

    cos_t, sin_t = _rope_tables(Lc, L)
    head = jnp.arange(ATTN_WIDTH) // HEAD_DIM
    ones = jnp.where(head[:, None] == head[None, :], 1.0 / HEAD_DIM, 0.0).astype(bf16)
    qn = (jnp.tile(q_norm, (1, N_HEADS)) * HEAD_DIM ** -0.5).reshape(depth, 1, ATTN_WIDTH)
    kn = jnp.tile(k_norm, (1, N_KV_HEADS)).reshape(depth, 1, KV_WIDTH)
    bmat, cmat, a_re, a_im = _ssm_matrices(ssm_a_re, ssm_a_im, ssm_log_dt, ssm_b_re, ssm_b_im,
                                           ssm_c_re, ssm_c_im, B)
    pool_bd = _block_diag(pool_w).astype(bf16)
    w_in_b, w_out_b, glu_w_b = w_in.astype(bf16), w_out.astype(bf16), ssm_glu_w.astype(bf16)
    w_gate_b, w_up_b, w_down_b = ffn_w_gate.astype(bf16), ffn_w_up.astype(bf16), ffn_w_down.astype(bf16)
    row = lambda a: a.reshape(depth, 1, a.shape[-1])

    xc = jnp.concatenate([ctx, x], axis=1)
    for l in range(depth):
        pool_u, u, q, k2, v2 = _inproj(l, xc, mod, row(norm_mix), w_in_b, cos_t, sin_t, qn, kn, ones)
        po = _pool(l, pool_u, pool_bd, row(pool_scale), Lc)
        y = _ssm(l, u, bmat, cmat, a_re, a_im, Lc)
        ao = _attention(l, attn_sink, q, k2, v2, Lc)
        xc = _outffn(l, xc, mod, po, u, y, ao, row(ssm_d), glu_w_b, row(ssm_glu_b), w_out_b,
                     row(norm_ffn), w_gate_b, w_up_b, w_down_b, skip_ctx=(l == depth - 1))
    return xc
```

```python
import functools
import math

import jax
import jax.numpy as jnp
from jax import lax
from jax.experimental import pallas as pl
from jax.experimental.pallas import tpu as pltpu

f32 = jnp.float32
bf16 = jnp.bfloat16

GRID_W = 64
POOL_WINDOWS = (2, 4, 8, 16)
POOL_GROUP = 64
POOL_WIDTH = POOL_GROUP * len(POOL_WINDOWS)
SSM_GROUP_CH = 16
SSM_GROUPS = 16
SSM_WIDTH = SSM_GROUP_CH * SSM_GROUPS
SSM_STATE = 64
SSM_LANES = SSM_GROUPS * SSM_STATE
HEAD_DIM = 64
N_HEADS = 8
N_KV_HEADS = 2
GROUP = N_HEADS // N_KV_HEADS
ATTN_WIDTH = N_HEADS * HEAD_DIM
KV_WIDTH = N_KV_HEADS * HEAD_DIM
WINDOW = 128
ROPE_BASE = 10000.0
ROPE_FREQS = HEAD_DIM // 4
N_MOD = 6
EPS = 1e-6

LANES = 128
SUBLANES = 8
MXU_TILE = 256
CHAIN_ROWS = 256
LATENT_BLOCK = 2 * CHAIN_ROWS
Q_TILE = 128
SCAN_STEPS = 64
SCAN_PITCH = SCAN_STEPS + 4
FF_CHUNK = 6 * MXU_TILE
VMEM_LIMIT_BYTES = 56 * 1024 * 1024


def _sigmoid(x):
    return 1.0 / (1.0 + jnp.exp(-x))


def _silu(x):
    return x * _sigmoid(x)


def _gelu_tanh(x):
    return 0.5 * x * (1.0 + jnp.tanh(math.sqrt(2.0 / math.pi) * (x + 0.044715 * (x * x * x))))


def _dot(a, b):
    return jnp.dot(a, b, preferred_element_type=f32)


def _dot_nt(a, b):
    return lax.dot_general(a, b, (((1,), (1,)), ((), ())), preferred_element_type=f32)


def _const_spec(shape):
    nd = len(shape)
    return pl.BlockSpec(shape, lambda *_: (0,) * nd, pipeline_mode=pl.Buffered(1))


def _layer_spec(shape, layer):
    nd = len(shape)
    return pl.BlockSpec((None,) + tuple(shape), lambda *_: (layer,) + (0,) * nd,
                        pipeline_mode=pl.Buffered(1))


def _params(*sem):
    return pltpu.CompilerParams(dimension_semantics=sem, vmem_limit_bytes=VMEM_LIMIT_BYTES)


def _chained(n_rows, head, tail):
    rows = [slice(r, r + CHAIN_ROWS) for r in range(0, n_rows, CHAIN_ROWS)]
    nxt = head(rows[0])
    for i, r in enumerate(rows):
        cur = nxt
        if i + 1 < len(rows):
            nxt = head(rows[i + 1])
        tail(r, *cur)


def _mod_kernel(c_ref, w_ref, b_ref, o_ref):
    a = _silu(c_ref[...]).astype(bf16)
    o_ref[...] = _dot(a, w_ref[...].astype(bf16)) + b_ref[...]


def _modulation(c16, w_mod, b_mod):
    depth, d, n = w_mod.shape
    tn = 1536
    return pl.pallas_call(
        _mod_kernel,
        out_shape=jax.ShapeDtypeStruct((depth, 16, n), f32),
        grid=(depth, n // tn),
        in_specs=[
            pl.BlockSpec((16, d), lambda l, j: (0, 0)),
            pl.BlockSpec((None, d, tn), lambda l, j: (l, 0, j)),
            pl.BlockSpec((None, 1, tn), lambda l, j: (l, 0, j)),
        ],
        out_specs=pl.BlockSpec((None, 16, tn), lambda l, j: (l, 0, j)),
        compiler_params=_params("arbitrary", "arbitrary"),
        name="modulation",
    )(c16, w_mod, b_mod.reshape(depth, 1, n))


def _rmsnorm_mod(x, g, scale, shift):
    y = x * lax.rsqrt(jnp.mean(x * x, axis=-1, keepdims=True) + EPS) * g
    return y * (1.0 + scale) + shift


def _rope(x, cos, sin_signed):
    lane = lax.broadcasted_iota(jnp.int32, x.shape, 1)
    first = (lane % 32) < 16
    partner = jnp.where(first, pltpu.roll(x, LANES - 16, 1), pltpu.roll(x, 16, 1))
    return x * cos + partner * sin_signed


def _inproj_kernel(x_ref, mod_ref, g_ref, w_ref, cos_ref, sin_ref, qn_ref, kn_ref, ones_ref,
                   *refs):
    pool_ref, ssm_ref, q_ref, k2_ref, v2_ref = refs[-5:]

    def norm(r):
        return (_rmsnorm_mod(x_ref[r, :], g_ref[...], mod_ref[1:2, :], mod_ref[0:1, :]).astype(bf16),)

    def project(r, h):
        proj = _dot(h, w_ref[...])
        pool_ref[r, :] = proj[:, 0:POOL_WIDTH]
        ssm_ref[r, :] = proj[:, POOL_WIDTH:POOL_WIDTH + SSM_WIDTH]
        o = POOL_WIDTH + SSM_WIDTH
        q = proj[:, o:o + ATTN_WIDTH]
        k = proj[:, o + ATTN_WIDTH:o + ATTN_WIDTH + KV_WIDTH]
        v = proj[:, o + ATTN_WIDTH + KV_WIDTH:o + ATTN_WIDTH + 2 * KV_WIDTH]
        cos = cos_ref[r, :]
        sin = sin_ref[r, :]
        q_ms = _dot((q * q).astype(bf16), ones_ref[...])
        qn = q * lax.rsqrt(q_ms + EPS) * qn_ref[...]
        for cblk in range(ATTN_WIDTH // LANES):
            sl = slice(cblk * LANES, (cblk + 1) * LANES)
            q_ref[r, sl] = _rope(qn[:, sl], cos, sin).astype(bf16)
        k_ms = _dot((k * k).astype(bf16), ones_ref[0:KV_WIDTH, 0:KV_WIDTH])
        kr = _rope(k * lax.rsqrt(k_ms + EPS) * kn_ref[...], cos, sin)
        lane = lax.broadcasted_iota(jnp.int32, kr.shape, 1)
        left = lane < HEAD_DIM
        ksw = pltpu.roll(kr, HEAD_DIM, 1)
        k2_ref[r, 0:LANES] = jnp.where(left, kr, ksw).astype(bf16)
        k2_ref[r, LANES:2 * LANES] = jnp.where(left, ksw, kr).astype(bf16)
        vsw = pltpu.roll(v, HEAD_DIM, 1)
        zero = jnp.zeros_like(v)
        v2_ref[r, 0:LANES] = jnp.where(left, v, zero).astype(bf16)
        v2_ref[r, LANES:2 * LANES] = jnp.where(left, zero, vsw).astype(bf16)
        v2_ref[r, 2 * LANES:3 * LANES] = jnp.where(left, vsw, zero).astype(bf16)
        v2_ref[r, 3 * LANES:4 * LANES] = jnp.where(left, zero, v).astype(bf16)

    _chained(x_ref.shape[0], norm, project)


def _inproj(layer, src, mod, mod_row, g, w_in, cos_t, sin_t, qn, kn, ones, n_total, row0, block,
            prev=None):
    B, n_src, D = src.shape
    j0 = row0 // block
    win = w_in.shape[2]
    widths = (POOL_WIDTH, SSM_WIDTH, ATTN_WIDTH, 2 * KV_WIDTH, 4 * KV_WIDTH)
    dtypes = (f32, f32, bf16, bf16, bf16)
    in_specs = [
        pl.BlockSpec((None, block, D), lambda b, j: (b, j, 0)),
        pl.BlockSpec((None, None, N_MOD, D), lambda b, j: (layer, mod_row(b), 0, 0)),
        _layer_spec((1, D), layer),
        _layer_spec((D, win), layer),
        pl.BlockSpec((block, LANES), lambda b, j: (j + j0, 0)),
        pl.BlockSpec((block, LANES), lambda b, j: (j + j0, 0)),
        _layer_spec((1, ATTN_WIDTH), layer),
        _layer_spec((1, KV_WIDTH), layer),
        _const_spec((ATTN_WIDTH, ATTN_WIDTH)),
    ]
    args = [src, mod, g, w_in, cos_t, sin_t, qn, kn, ones]
    aliases = {}
    if prev is not None:
        aliases = {len(args) + i: i for i in range(len(prev))}
        in_specs += [pl.BlockSpec(memory_space=pl.ANY)] * len(prev)
        args += list(prev)
    return pl.pallas_call(
        _inproj_kernel,
        out_shape=tuple(jax.ShapeDtypeStruct((B, n_total, w), dt) for w, dt in zip(widths, dtypes)),
        grid=(B, n_src // block),
        in_specs=in_specs,
        out_specs=tuple(pl.BlockSpec((None, block, w), lambda b, j: (b, j + j0, 0)) for w in widths),
        input_output_aliases=aliases,
        compiler_params=_params("arbitrary", "arbitrary"),
        name="inproj",
    )(*args)


def _shift_rows(x, s, row):
    n = x.shape[0]
    if s > 0:
        return jnp.where(row >= s, pltpu.roll(x, s, 0), 0.0)
    return jnp.where(row < n + s, pltpu.roll(x, n + s, 0), 0.0)


def _window_sums(u, half_a, half_b, lane_split):
    row = lax.broadcasted_iota(jnp.int32, u.shape, 0)
    lane = lax.broadcasted_iota(jnp.int32, u.shape, 1)
    bwd = u
    fwd = u
    k = 1
    out_a = None
    while True:
        if k == half_a:
            out_a = _shift_rows(bwd, 1, row) + fwd
        if k == half_b:
            out_b = _shift_rows(bwd, 1, row) + fwd
            break
        bwd = bwd + _shift_rows(bwd, k, row)
        fwd = fwd + _shift_rows(fwd, -k, row)
        k *= 2
    n = u.shape[0]
    half = jnp.where(lane < lane_split, half_a, half_b)
    cnt = jnp.minimum(row + half, n) - jnp.maximum(row - half, 0)
    return jnp.where(lane < lane_split, out_a, out_b) / cnt.astype(f32)


def _pool_segment(u_ref, w_ref, scale_ref, o_ref, r0, n):
    parts = []
    for blk in range(POOL_WIDTH // LANES):
        u = u_ref[r0:r0 + n, blk * LANES:(blk + 1) * LANES]
        wa, wb = POOL_WINDOWS[2 * blk], POOL_WINDOWS[2 * blk + 1]
        mean = _window_sums(u, wa // 2, wb // 2, POOL_GROUP)
        parts.append((mean - u).astype(bf16))
    d = jnp.concatenate(parts, axis=1)
    o_ref[r0:r0 + n, :] = (_dot(d, w_ref[...]) * scale_ref[...]).astype(bf16)


def _pool_kernel(u_ref, w_ref, scale_ref, o_ref, *, n_lat):
    nt = u_ref.shape[0]
    _pool_segment(u_ref, w_ref, scale_ref, o_ref, 0, n_lat)
    _pool_segment(u_ref, w_ref, scale_ref, o_ref, n_lat, nt - n_lat)


def _pool(layer, pool_u, w_bd, scale, n_lat):
    B, NT, W = pool_u.shape
    return pl.pallas_call(
        functools.partial(_pool_kernel, n_lat=n_lat),
        out_shape=jax.ShapeDtypeStruct((B, NT, W), bf16),
        grid=(B,),
        in_specs=[
            pl.BlockSpec((None, NT, W), lambda b: (b, 0, 0)),
            _layer_spec((W, W), layer),
            _layer_spec((1, W), layer),
        ],
        out_specs=pl.BlockSpec((None, NT, W), lambda b: (b, 0, 0)),
        compiler_params=_params("arbitrary"),
        name="pool",
    )(pool_u, w_bd, scale)


def _ssm_kernel(u_ref, bmat_ref, cmat_ref, are_ref, aim_ref, y_ref,
                upad_ref, bu0_ref, bu1_ref, s0_ref, s1_ref, state_ref, *, n_chunks):
    g = pl.program_id(0)
    nb = u_ref.shape[0]
    ts, pitch = SCAN_STEPS, SCAN_PITCH
    n_seq = 2 * n_chunks
    nk = SSM_LANES // LANES

    @pl.when(g == 0)
    def _():
        for r in (upad_ref, bu0_ref, bu1_ref, s0_ref, s1_ref, state_ref):
            r[...] = jnp.zeros_like(r)

    def direction(i):
        return jnp.clip(i, 0, n_seq - 1) // n_chunks

    def half_rows(i, h):
        off = jnp.where(direction(i) == 0, h, 1 - h) * ts
        return pl.ds(pl.multiple_of(off, ts), ts)

    def stage_b(i, h, bu_ref):
        rows = half_rows(i, h)
        for b in range(nb):
            upad_ref[b * pitch:b * pitch + ts, :] = u_ref[b, rows, :]
        res = _dot(upad_ref[...].astype(bf16), bmat_ref[direction(i)])
        for k in range(2 * nk):
            bu_ref[k] = res[:, k * LANES:(k + 1) * LANES]

    def stage_scan(i, bu_ref, s_ref):
        ic = jnp.clip(i, 0, n_seq - 1)
        d = ic // n_chunks
        keep = jnp.where((ic % n_chunks) == 0, 0.0, 1.0)
        a_re = are_ref[d]
        a_im = aim_ref[d]
        st_re = [state_ref[:, k * LANES:(k + 1) * LANES] * keep for k in range(nk)]
        st_im = [state_ref[:, (nk + k) * LANES:(nk + k + 1) * LANES] * keep for k in range(nk)]
        for t in range(ts):
            tt = jnp.where(d == 0, t, ts - 1 - t)
            rows = pl.ds(tt, nb, stride=pitch)
            for k in range(nk):
                ar = a_re[:, k * LANES:(k + 1) * LANES]
                ai = a_im[:, k * LANES:(k + 1) * LANES]
                n_re = ar * st_re[k] - ai * st_im[k] + bu_ref[k, rows, :]
                n_im = ar * st_im[k] + ai * st_re[k] + bu_ref[nk + k, rows, :]
                s_ref[k, rows, :] = n_re
                s_ref[nk + k, rows, :] = n_im
                st_re[k], st_im[k] = n_re, n_im
        for k in range(nk):
            state_ref[:, k * LANES:(k + 1) * LANES] = st_re[k]
            state_ref[:, (nk + k) * LANES:(nk + k + 1) * LANES] = st_im[k]

    def stage_c(i, h, s_ref):
        s = jnp.concatenate([s_ref[k] for k in range(2 * nk)], axis=1).astype(bf16)
        yv = _dot(s, cmat_ref[direction(i)])
        rows = half_rows(i, h)
        for b in range(nb):
            y_ref[b, rows, :] = yv[b * pitch:b * pitch + ts, :]

    stage_b(2 * g, 0, bu0_ref)
    stage_c(2 * g - 2, 0, s0_ref)
    stage_scan(2 * g - 1, bu1_ref, s1_ref)
    stage_b(2 * g + 1, 1, bu1_ref)
    stage_scan(2 * g, bu0_ref, s0_ref)
    stage_c(2 * g - 1, 1, s1_ref)


def _ssm(layer, u, bmat, cmat, a_re, a_im, n_lat):
    B, NT, W = u.shape
    ts = SCAN_STEPS
    n_chunks = NT // ts
    n_blocks = n_chunks // 2
    nx = n_lat // (2 * ts)
    nc = n_blocks - nx

    def block(step):
        step = jnp.clip(step, 0, 2 * n_blocks - 1)
        d = step // n_blocks
        j = step % n_blocks
        fwd = jnp.where(j < nc, nx + j, j - nc)
        back = jnp.where(j < nc, n_blocks - 1 - j, nx - 1 - (j - nc))
        return d, jnp.where(d == 0, fwd, back)

    rows = B * SCAN_PITCH
    slab = pltpu.VMEM((2 * SSM_LANES // LANES, rows, LANES), f32)
    return pl.pallas_call(
        functools.partial(_ssm_kernel, n_chunks=n_chunks),
        out_shape=jax.ShapeDtypeStruct((2, B, NT, W), f32),
        grid=(2 * n_blocks + 1,),
        in_specs=[
            pl.BlockSpec((B, 2 * ts, W), lambda g: (0, block(g)[1], 0)),
            _layer_spec((2, W, 2 * SSM_LANES), layer),
            _layer_spec((2, 2 * SSM_LANES, W), layer),
            _layer_spec((2, B, SSM_LANES), layer),
            _layer_spec((2, B, SSM_LANES), layer),
        ],
        out_specs=pl.BlockSpec((None, B, 2 * ts, W),
                               lambda g: (block(g - 1)[0], 0, block(g - 1)[1], 0)),
        scratch_shapes=[pltpu.VMEM((rows, W), f32), slab, slab, slab, slab,
                        pltpu.VMEM((B, 2 * SSM_LANES), f32)],
        compiler_params=_params("arbitrary"),
        name="ssm_scan",
    )(u, bmat, cmat, a_re, a_im)


def _attend_pair(qp, keys, vals_l, vals_r, masks, sink_a, sink_b):
    nq = qp.shape[0]
    lane = lax.broadcasted_iota(jnp.int32, qp.shape, 1)
    zero = jnp.zeros_like(qp)
    lhs = jnp.concatenate([jnp.where(lane < HEAD_DIM, qp, zero),
                           jnp.where(lane < HEAD_DIM, zero, qp)], axis=0)
    row = lax.broadcasted_iota(jnp.int32, (2 * nq, 1), 0)
    sink = jnp.where(row < nq, sink_a, sink_b)
    scores = []
    m = sink
    for kk, mk in zip(keys, masks):
        s = _dot_nt(lhs, kk)
        if mk is not None:
            s = jnp.where(jnp.concatenate([mk, mk], axis=0), s, -jnp.inf)
        scores.append(s)
        m = jnp.maximum(m, jnp.max(s, axis=-1, keepdims=True))
    den = jnp.exp(sink - m)
    out_a = jnp.zeros((nq, LANES), f32)
    out_b = jnp.zeros((nq, LANES), f32)
    for s, vl, vr in zip(scores, vals_l, vals_r):
        p = jnp.exp(s - m)
        den = den + jnp.sum(p, axis=-1, keepdims=True)
        pb = p.astype(bf16)
        out_a = out_a + _dot(pb[0:nq], vl)
        out_b = out_b + _dot(pb[nq:2 * nq], vr)
    return out_a / den[0:nq] + out_b / den[nq:2 * nq]


def _attn_kernel(sink_ref, q_ref, k2_ref, v2_ref, o_ref, *, n_lat):
    i = pl.program_id(1)
    nq = q_ref.shape[0]
    n_ctx = k2_ref.shape[0] - n_lat
    n_lat_tiles = n_lat // nq
    span = 3 * WINDOW

    def run(keys_rows, masks):
        for hk in range(N_KV_HEADS):
            keys = [k2_ref[r, hk * LANES:(hk + 1) * LANES] for r in keys_rows]
            vals_l = [v2_ref[r, 2 * hk * LANES:(2 * hk + 1) * LANES] for r in keys_rows]
            vals_r = [v2_ref[r, (2 * hk + 1) * LANES:(2 * hk + 2) * LANES] for r in keys_rows]
            for p in range(GROUP // 2):
                c0 = (hk * (GROUP // 2) + p) * LANES
                h_a = hk * GROUP + 2 * p
                out = _attend_pair(q_ref[:, c0:c0 + LANES], keys, vals_l, vals_r, masks,
                                   sink_ref[h_a], sink_ref[h_a + 1])
                o_ref[:, c0:c0 + LANES] = out.astype(bf16)

    ctx_rows = pl.ds(n_lat, n_ctx)

    @pl.when(i >= n_lat_tiles)
    def _():
        run([ctx_rows], [None])

    @pl.when(i < n_lat_tiles)
    def _():
        q0 = i * nq
        start = jnp.clip(q0 - WINDOW, 0, n_lat - span)
        q_pos = q0 + lax.broadcasted_iota(jnp.int32, (nq, span), 0)
        k_pos = start + lax.broadcasted_iota(jnp.int32, (nq, span), 1)
        valid = jnp.abs(k_pos - q_pos) <= WINDOW
        win_rows = pl.ds(pl.multiple_of(start, WINDOW), span)
        run([win_rows, ctx_rows], [valid, None])


def _attention(layer, sink, q, k2, v2, n_lat):
    B, NT, _ = q.shape
    return pl.pallas_call(
        functools.partial(_attn_kernel, n_lat=n_lat),
        out_shape=jax.ShapeDtypeStruct((B, NT, ATTN_WIDTH), bf16),
        grid=(B, NT // Q_TILE),
        in_specs=[
            pl.BlockSpec(memory_space=pltpu.SMEM),
            pl.BlockSpec((None, Q_TILE, ATTN_WIDTH), lambda b, i: (b, i, 0)),
            pl.BlockSpec((None, NT, 2 * KV_WIDTH), lambda b, i: (b, 0, 0)),
            pl.BlockSpec((None, NT, 4 * KV_WIDTH), lambda b, i: (b, 0, 0)),
        ],
        out_specs=pl.BlockSpec((None, Q_TILE, ATTN_WIDTH), lambda b, i: (b, i, 0)),
        compiler_params=_params("arbitrary", "arbitrary"),
        name="attention",
    )(sink[layer], q, k2, v2)


def _outffn_kernel(x_ref, mod_ref, po_ref, u_ref, y0_ref, y1_ref, ao_ref, d_ref, gw_ref, gb_ref,
                   wo_ref, g_ref, wg_ref, wu_ref, wd_ref, o_ref):
    o1 = POOL_WIDTH
    o2 = POOL_WIDTH + SSM_WIDTH
    d_ff = wg_ref.shape[1]

    def mixer(r):
        y = d_ref[...] * u_ref[r, :] + y0_ref[r, :] + y1_ref[r, :]
        ge = _gelu_tanh(y)
        so = ge * _sigmoid(_dot(ge.astype(bf16), gw_ref[...]) + gb_ref[...])
        mix = (_dot(po_ref[r, :], wo_ref[0:o1, :]) + _dot(so.astype(bf16), wo_ref[o1:o2, :])
               + _dot(ao_ref[r, :], wo_ref[o2:, :]))
        x1 = x_ref[r, :] + mod_ref[2:3, :] * mix
        return x1, _rmsnorm_mod(x1, g_ref[...], mod_ref[4:5, :], mod_ref[3:4, :]).astype(bf16)

    def ffn(r, x1, h):
        acc = jnp.zeros_like(x1)
        for c0 in range(0, d_ff, FF_CHUNK):
            c1 = min(c0 + FF_CHUNK, d_ff)
            gate = _dot(h, wg_ref[:, c0:c1])
            up = _dot(h, wu_ref[:, c0:c1])
            acc = acc + _dot((_silu(gate) * up).astype(bf16), wd_ref[c0:c1, :])
        o_ref[r, :] = x1 + mod_ref[5:6, :] * acc

    _chained(x_ref.shape[0], mixer, ffn)


def _outffn(layer, src, mod, mod_row, po, u, y, ao, d_skip, glu_w, glu_b, w_out, g, w_gate, w_up,
            w_down, row0, block):
    B, n_src, D = src.shape
    j0 = row0 // block
    d_ff = w_gate.shape[2]
    tok = lambda w: pl.BlockSpec((None, block, w), lambda b, j: (b, j + j0, 0))
    ydir = lambda d: pl.BlockSpec((None, None, block, SSM_WIDTH), lambda b, j: (d, b, j + j0, 0))
    return pl.pallas_call(
        _outffn_kernel,
        out_shape=jax.ShapeDtypeStruct((B, n_src, D), f32),
        grid=(B, n_src // block),
        in_specs=[
            pl.BlockSpec((None, block, D), lambda b, j: (b, j, 0)),
            pl.BlockSpec((None, None, N_MOD, D), lambda b, j: (layer, mod_row(b), 0, 0)),
            tok(POOL_WIDTH),
            tok(SSM_WIDTH),
            ydir(0),
            ydir(1),
            tok(ATTN_WIDTH),
            _layer_spec((1, SSM_WIDTH), layer),
            _layer_spec((SSM_WIDTH, SSM_WIDTH), layer),
            _layer_spec((1, SSM_WIDTH), layer),
            _layer_spec((D, D), layer),
            _layer_spec((1, D), layer),
            _layer_spec((D, d_ff), layer),
            _layer_spec((D, d_ff), layer),
            _layer_spec((d_ff, D), layer),
        ],
        out_specs=pl.BlockSpec((None, block, D), lambda b, j: (b, j, 0)),
        compiler_params=_params("arbitrary", "arbitrary"),
        name="outproj_ffn",
    )(src, mod, po, u, y, y, ao, d_skip, glu_w, glu_b, w_out, g, w_gate, w_up, w_down)


def _rope_tables(n_lat, n_ctx):
    pos = jnp.arange(n_lat)
    row = (pos // GRID_W).astype(f32)
    col = (pos % GRID_W).astype(f32)
    inv = jnp.power(ROPE_BASE, -jnp.arange(ROPE_FREQS, dtype=f32) / ROPE_FREQS)
    lane = jnp.arange(LANES)
    dim = lane % HEAD_DIM
    freq = inv[dim % ROPE_FREQS]
    ang = jnp.where((dim // (2 * ROPE_FREQS)) == 0, row[:, None], col[:, None]) * freq[None, :]
    sign = jnp.where((dim % (2 * ROPE_FREQS)) < ROPE_FREQS, -1.0, 1.0).astype(f32)
    cos = jnp.concatenate([jnp.cos(ang), jnp.ones((n_ctx, LANES), f32)], axis=0)
    sin = jnp.concatenate([jnp.sin(ang) * sign, jnp.zeros((n_ctx, LANES), f32)], axis=0)
    return cos, sin


def _block_diag(blocks):
    *lead, g, r, c = blocks.shape
    eye = jnp.eye(g, dtype=blocks.dtype)
    return jnp.einsum('...grc,gh->...grhc', blocks, eye).reshape(*lead, g * r, g * c)


def _ssm_matrices(a_re, a_im, log_dt, b_re, b_im, c_re, c_im, n_batch):
    lr = jnp.minimum(a_re.astype(f32), -1e-4)
    li = a_im.astype(f32)
    dt = jnp.exp(log_dt.astype(f32))[..., None]
    mag = jnp.exp(lr * dt)
    ar = mag * jnp.cos(li * dt)
    ai = mag * jnp.sin(li * dt)
    den = lr * lr + li * li
    qr = ((ar - 1.0) * lr + ai * li) / den
    qi = (ai * lr - (ar - 1.0) * li) / den
    bbr = qr[..., None] * b_re - qi[..., None] * b_im
    bbi = qr[..., None] * b_im + qi[..., None] * b_re
    bmat = jnp.concatenate([_block_diag(jnp.swapaxes(bbr, -1, -2)),
                            _block_diag(jnp.swapaxes(bbi, -1, -2))], axis=-1)
    cmat = jnp.concatenate([_block_diag(jnp.swapaxes(c_re.astype(f32), -1, -2)),
                            -_block_diag(jnp.swapaxes(c_im.astype(f32), -1, -2))], axis=-2)
    shape = a_re.shape[:2] + (n_batch, SSM_LANES)
    a_re_b = jnp.broadcast_to(ar.reshape(a_re.shape[:2] + (1, SSM_LANES)), shape)
    a_im_b = jnp.broadcast_to(ai.reshape(a_re.shape[:2] + (1, SSM_LANES)), shape)
    return bmat.astype(bf16), cmat.astype(bf16), a_re_b, a_im_b


def kernel(x, c, ctx, c_ctx, w_mod, b_mod, norm_mix, norm_ffn, w_in, w_out, pool_w, pool_scale, ssm_a_re, ssm_a_im, ssm_log_dt, ssm_b_re, ssm_b_im, ssm_c_re, ssm_c_im, ssm_d, ssm_glu_w, ssm_glu_b, q_norm, k_norm, attn_sink, ffn_w_gate, ffn_w_up, ffn_w_down):
    B, L, D = x.shape
    Lc = ctx.shape[1]
    NT = L + Lc
    depth = w_mod.shape[0]
    assert Lc == CHAIN_ROWS and L % LATENT_BLOCK == 0 and L % GRID_W == 0 and L >= 3 * WINDOW
    assert B == SUBLANES and w_in.shape[2] == POOL_WIDTH + SSM_WIDTH + ATTN_WIDTH + 2 * KV_WIDTH
    assert w_mod.shape[2] == N_MOD * D and Lc % (2 * SCAN_STEPS) == 0 and L % (2 * SCAN_STEPS) == 0

    c16 = jnp.concatenate([c, c_ctx[None, :], jnp.zeros((16 - B - 1, D), f32)], axis=0)
    mod = _modulation(c16, w_mod, b_mod).reshape(depth, 16, N_MOD, D)
    lat_row = lambda b: b
    ctx_row = lambda b: B

    cos_t, sin_t = _rope_tables(L, Lc)
    head = jnp.arange(ATTN_WIDTH) // HEAD_DIM
    ones = jnp.where(head[:, None] == head[None, :], 1.0 / HEAD_DIM, 0.0).astype(bf16)
    qn = (jnp.tile(q_norm, (1, N_HEADS)) * HEAD_DIM ** -0.5).reshape(depth, 1, ATTN_WIDTH)
    kn = jnp.tile(k_norm, (1, N_KV_HEADS)).reshape(depth, 1, KV_WIDTH)
    bmat, cmat, a_re, a_im = _ssm_matrices(ssm_a_re, ssm_a_im, ssm_log_dt, ssm_b_re, ssm_b_im,
                                           ssm_c_re, ssm_c_im, B)
    pool_bd = _block_diag(pool_w).astype(bf16)
    w_in_b, w_out_b, glu_w_b = w_in.astype(bf16), w_out.astype(bf16), ssm_glu_w.astype(bf16)
    w_gate_b, w_up_b, w_down_b = ffn_w_gate.astype(bf16), ffn_w_up.astype(bf16), ffn_w_down.astype(bf16)
    row = lambda a: a.reshape(depth, 1, a.shape[-1])

    for l in range(depth):
        proj_args = (row(norm_mix), w_in_b, cos_t, sin_t, qn, kn, ones, NT)
        outs = _inproj(l, x, mod, lat_row, *proj_args, 0, LATENT_BLOCK)
        pool_u, u, q, k2, v2 = _inproj(l, ctx, mod, ctx_row, *proj_args, L, CHAIN_ROWS, prev=outs)
        po = _pool(l, pool_u, pool_bd, row(pool_scale), L)
        y = _ssm(l, u, bmat, cmat, a_re, a_im, L)
        ao = _attention(l, attn_sink, q, k2, v2, L)
        ffn_args = (po, u, y, ao, row(ssm_d), glu_w_b, row(ssm_glu_b), w_out_b, row(norm_ffn),
                    w_gate_b, w_up_b, w_down_b)
        x_new = _outffn(l, x, mod, lat_row, *ffn_args, 0, LATENT_BLOCK)
        if l < depth - 1:
            ctx = _outffn(l, ctx, mod, ctx_row, *ffn_args, L, CHAIN_ROWS)
        x = x_new
    return x
```

```python
import functools
import math

import jax
import jax.numpy as jnp
from jax import lax
from jax.experimental import pallas as pl
from jax.experimental.pallas import tpu as pltpu

f32 = jnp.float32
bf16 = jnp.bfloat16

GRID_W = 64
POOL_WINDOWS = (2, 4, 8, 16)
POOL_GROUP = 64
POOL_WIDTH = POOL_GROUP * len(POOL_WINDOWS)
SSM_GROUP_CH = 16
SSM_GROUPS = 16
SSM_WIDTH = SSM_GROUP_CH * SSM_GROUPS
SSM_STATE = 64
SSM_LANES = SSM_GROUPS * SSM_STATE
HEAD_DIM = 64
N_HEADS = 8
N_KV_HEADS = 2
GROUP = N_HEADS // N_KV_HEADS
ATTN_WIDTH = N_HEADS * HEAD_DIM
KV_WIDTH = N_KV_HEADS * HEAD_DIM
WINDOW = 128
ROPE_BASE = 10000.0
ROPE_FREQS = HEAD_DIM // 4
N_MOD = 6
EPS = 1e-6
LOG2_E = math.log2(math.e)

LANES = 128
SUBLANES = 8
MXU_TILE = 256
CHAIN_ROWS = 256
LATENT_BLOCK = 2 * CHAIN_ROWS
Q_TILE = 128
SCAN_STEPS = 64
SCAN_PITCH = SCAN_STEPS + 4
FF_CHUNK = 6 * MXU_TILE
VMEM_LIMIT_BYTES = 56 * 1024 * 1024


def _sigmoid(x):
    return 1.0 / (1.0 + jnp.exp(-x))


def _silu(x):
    return x * _sigmoid(x)


def _gelu_tanh(x):
    return 0.5 * x * (1.0 + jnp.tanh(math.sqrt(2.0 / math.pi) * (x + 0.044715 * (x * x * x))))


def _dot(a, b):
    return jnp.dot(a, b, preferred_element_type=f32)


def _dot_nt(a, b):
    return lax.dot_general(a, b, (((1,), (1,)), ((), ())), preferred_element_type=f32)


def _const_spec(shape):
    nd = len(shape)
    return pl.BlockSpec(shape, lambda *_: (0,) * nd, pipeline_mode=pl.Buffered(1))


def _layer_spec(shape, layer):
    nd = len(shape)
    return pl.BlockSpec((None,) + tuple(shape), lambda *_: (layer,) + (0,) * nd,
                        pipeline_mode=pl.Buffered(1))


def _params(*sem):
    return pltpu.CompilerParams(dimension_semantics=sem, vmem_limit_bytes=VMEM_LIMIT_BYTES)


def _chained(n_rows, head, tail):
    rows = [slice(r, r + CHAIN_ROWS) for r in range(0, n_rows, CHAIN_ROWS)]
    nxt = head(rows[0])
    for i, r in enumerate(rows):
        cur = nxt
        if i + 1 < len(rows):
            nxt = head(rows[i + 1])
        tail(r, *cur)


def _mod_kernel(c_ref, w_ref, b_ref, o_ref):
    a = _silu(c_ref[...]).astype(bf16)
    o_ref[...] = _dot(a, w_ref[...].astype(bf16)) + b_ref[...]


def _modulation(c16, w_mod, b_mod):
    depth, d, n = w_mod.shape
    tn = 1536
    return pl.pallas_call(
        _mod_kernel,
        out_shape=jax.ShapeDtypeStruct((depth, 16, n), f32),
        grid=(depth, n // tn),
        in_specs=[
            pl.BlockSpec((16, d), lambda l, j: (0, 0)),
            pl.BlockSpec((None, d, tn), lambda l, j: (l, 0, j)),
            pl.BlockSpec((None, 1, tn), lambda l, j: (l, 0, j)),
        ],
        out_specs=pl.BlockSpec((None, 16, tn), lambda l, j: (l, 0, j)),
        compiler_params=_params("arbitrary", "arbitrary"),
        name="modulation",
    )(c16, w_mod, b_mod.reshape(depth, 1, n))


def _rmsnorm_mod(x, g, scale, shift):
    y = x * lax.rsqrt(jnp.mean(x * x, axis=-1, keepdims=True) + EPS) * g
    return y * (1.0 + scale) + shift


def _rope(x, cos, sin_signed):
    lane = lax.broadcasted_iota(jnp.int32, x.shape, 1)
    first = (lane % 32) < 16
    partner = jnp.where(first, pltpu.roll(x, LANES - 16, 1), pltpu.roll(x, 16, 1))
    return x * cos + partner * sin_signed


def _inproj_kernel(x_ref, mod_ref, g_ref, w_ref, cos_ref, sin_ref, qn_ref, kn_ref, ones_ref,
                   *refs):
    pool_ref, ssm_ref, q_ref, k2_ref, v2_ref = refs[-5:]

    def norm(r):
        return (_rmsnorm_mod(x_ref[r, :], g_ref[...], mod_ref[1:2, :], mod_ref[0:1, :]).astype(bf16),)

    def project(r, h):
        proj = _dot(h, w_ref[...])
        pool_ref[r, :] = proj[:, 0:POOL_WIDTH]
        ssm_ref[r, :] = proj[:, POOL_WIDTH:POOL_WIDTH + SSM_WIDTH]
        o = POOL_WIDTH + SSM_WIDTH
        q = proj[:, o:o + ATTN_WIDTH]
        k = proj[:, o + ATTN_WIDTH:o + ATTN_WIDTH + KV_WIDTH]
        v = proj[:, o + ATTN_WIDTH + KV_WIDTH:o + ATTN_WIDTH + 2 * KV_WIDTH]
        cos = cos_ref[r, :]
        sin = sin_ref[r, :]
        q_ms = _dot((q * q).astype(bf16), ones_ref[...])
        qn = q * lax.rsqrt(q_ms + EPS) * qn_ref[...]
        for cblk in range(ATTN_WIDTH // LANES):
            sl = slice(cblk * LANES, (cblk + 1) * LANES)
            q_ref[r, sl] = _rope(qn[:, sl], cos, sin).astype(bf16)
        k_ms = _dot((k * k).astype(bf16), ones_ref[0:KV_WIDTH, 0:KV_WIDTH])
        kr = _rope(k * lax.rsqrt(k_ms + EPS) * kn_ref[...], cos, sin)
        lane = lax.broadcasted_iota(jnp.int32, kr.shape, 1)
        left = lane < HEAD_DIM
        ksw = pltpu.roll(kr, HEAD_DIM, 1)
        k2_ref[r, 0:LANES] = jnp.where(left, kr, ksw).astype(bf16)
        k2_ref[r, LANES:2 * LANES] = jnp.where(left, ksw, kr).astype(bf16)
        vsw = pltpu.roll(v, HEAD_DIM, 1)
        ones_col = jnp.where(lane == HEAD_DIM, 1.0, 0.0)
        v2_ref[r, 0:LANES] = jnp.where(left, v, ones_col).astype(bf16)
        v2_ref[r, LANES:2 * LANES] = jnp.where(left, vsw, ones_col).astype(bf16)

    _chained(x_ref.shape[0], norm, project)


def _inproj(layer, src, mod, mod_row, g, w_in, cos_t, sin_t, qn, kn, ones, n_total, row0, block,
            prev=None):
    B, n_src, D = src.shape
    j0 = row0 // block
    win = w_in.shape[2]
    widths = (POOL_WIDTH, SSM_WIDTH, ATTN_WIDTH, 2 * KV_WIDTH, 2 * KV_WIDTH)
    dtypes = (f32, f32, bf16, bf16, bf16)
    in_specs = [
        pl.BlockSpec((None, block, D), lambda b, j: (b, j, 0)),
        pl.BlockSpec((None, None, N_MOD, D), lambda b, j: (layer, mod_row(b), 0, 0)),
        _layer_spec((1, D), layer),
        _layer_spec((D, win), layer),
        pl.BlockSpec((block, LANES), lambda b, j: (j + j0, 0)),
        pl.BlockSpec((block, LANES), lambda b, j: (j + j0, 0)),
        _layer_spec((1, ATTN_WIDTH), layer),
        _layer_spec((1, KV_WIDTH), layer),
        _const_spec((ATTN_WIDTH, ATTN_WIDTH)),
    ]
    args = [src, mod, g, w_in, cos_t, sin_t, qn, kn, ones]
    aliases = {}
    if prev is not None:
        aliases = {len(args) + i: i for i in range(len(prev))}
        in_specs += [pl.BlockSpec(memory_space=pl.ANY)] * len(prev)
        args += list(prev)
    return pl.pallas_call(
        _inproj_kernel,
        out_shape=tuple(jax.ShapeDtypeStruct((B, n_total, w), dt) for w, dt in zip(widths, dtypes)),
        grid=(B, n_src // block),
        in_specs=in_specs,
        out_specs=tuple(pl.BlockSpec((None, block, w), lambda b, j: (b, j + j0, 0)) for w in widths),
        input_output_aliases=aliases,
        compiler_params=_params("arbitrary", "arbitrary"),
        name="inproj",
    )(*args)


def _shift_rows(x, s, row):
    n = x.shape[0]
    if s > 0:
        return jnp.where(row >= s, pltpu.roll(x, s, 0), 0.0)
    return jnp.where(row < n + s, pltpu.roll(x, n + s, 0), 0.0)


def _window_sums(u, half_a, half_b, lane_split):
    row = lax.broadcasted_iota(jnp.int32, u.shape, 0)
    lane = lax.broadcasted_iota(jnp.int32, u.shape, 1)
    bwd = u
    fwd = u
    k = 1
    out_a = None
    while True:
        if k == half_a:
            out_a = _shift_rows(bwd, 1, row) + fwd
        if k == half_b:
            out_b = _shift_rows(bwd, 1, row) + fwd
            break
        bwd = bwd + _shift_rows(bwd, k, row)
        fwd = fwd + _shift_rows(fwd, -k, row)
        k *= 2
    n = u.shape[0]
    half = jnp.where(lane < lane_split, half_a, half_b)
    cnt = jnp.minimum(row + half, n) - jnp.maximum(row - half, 0)
    return jnp.where(lane < lane_split, out_a, out_b) / cnt.astype(f32)


def _pool_segment(u_ref, w_ref, scale_ref, o_ref, r0, n):
    parts = []
    for blk in range(POOL_WIDTH // LANES):
        u = u_ref[r0:r0 + n, blk * LANES:(blk + 1) * LANES]
        wa, wb = POOL_WINDOWS[2 * blk], POOL_WINDOWS[2 * blk + 1]
        mean = _window_sums(u, wa // 2, wb // 2, POOL_GROUP)
        parts.append((mean - u).astype(bf16))
    d = jnp.concatenate(parts, axis=1)
    o_ref[r0:r0 + n, :] = (_dot(d, w_ref[...]) * scale_ref[...]).astype(bf16)


def _pool_kernel(u_ref, w_ref, scale_ref, o_ref, *, n_lat):
    nt = u_ref.shape[0]
    _pool_segment(u_ref, w_ref, scale_ref, o_ref, 0, n_lat)
    _pool_segment(u_ref, w_ref, scale_ref, o_ref, n_lat, nt - n_lat)


def _pool(layer, pool_u, w_bd, scale, n_lat):
    B, NT, W = pool_u.shape
    return pl.pallas_call(
        functools.partial(_pool_kernel, n_lat=n_lat),
        out_shape=jax.ShapeDtypeStruct((B, NT, W), bf16),
        grid=(B,),
        in_specs=[
            pl.BlockSpec((None, NT, W), lambda b: (b, 0, 0)),
            _layer_spec((W, W), layer),
            _layer_spec((1, W), layer),
        ],
        out_specs=pl.BlockSpec((None, NT, W), lambda b: (b, 0, 0)),
        compiler_params=_params("arbitrary"),
        name="pool",
    )(pool_u, w_bd, scale)


def _ssm_kernel(u_ref, bmat_ref, cmat_ref, are_ref, aim_ref, y_ref,
                upad_ref, bu0_ref, bu1_ref, s0_ref, s1_ref, state_ref, *, n_chunks):
    g = pl.program_id(0)
    nb = u_ref.shape[0]
    ts, pitch = SCAN_STEPS, SCAN_PITCH
    n_seq = 2 * n_chunks
    nk = SSM_LANES // LANES

    @pl.when(g == 0)
    def _():
        for r in (upad_ref, bu0_ref, bu1_ref, s0_ref, s1_ref, state_ref):
            r[...] = jnp.zeros_like(r)

    def direction(i):
        return jnp.clip(i, 0, n_seq - 1) // n_chunks

    def half_rows(i, h):
        off = jnp.where(direction(i) == 0, h, 1 - h) * ts
        return pl.ds(pl.multiple_of(off, ts), ts)

    def stage_b(i, h, bu_ref):
        rows = half_rows(i, h)
        for b in range(nb):
            upad_ref[b * pitch:b * pitch + ts, :] = u_ref[b, rows, :]
        res = _dot(upad_ref[...].astype(bf16), bmat_ref[direction(i)])
        for k in range(2 * nk):
            bu_ref[k] = res[:, k * LANES:(k + 1) * LANES]

    def stage_scan(i, bu_ref, s_ref):
        ic = jnp.clip(i, 0, n_seq - 1)
        d = ic // n_chunks
        keep = jnp.where((ic % n_chunks) == 0, 0.0, 1.0)
        a_re = are_ref[d]
        a_im = aim_ref[d]
        st_re = [state_ref[:, k * LANES:(k + 1) * LANES] * keep for k in range(nk)]
        st_im = [state_ref[:, (nk + k) * LANES:(nk + k + 1) * LANES] * keep for k in range(nk)]
        for t in range(ts):
            tt = jnp.where(d == 0, t, ts - 1 - t)
            rows = pl.ds(tt, nb, stride=pitch)
            for k in range(nk):
                ar = a_re[:, k * LANES:(k + 1) * LANES]
                ai = a_im[:, k * LANES:(k + 1) * LANES]
                n_re = ar * st_re[k] - ai * st_im[k] + bu_ref[k, rows, :]
                n_im = ar * st_im[k] + ai * st_re[k] + bu_ref[nk + k, rows, :]
                s_ref[k, rows, :] = n_re
                s_ref[nk + k, rows, :] = n_im
                st_re[k], st_im[k] = n_re, n_im
        for k in range(nk):
            state_ref[:, k * LANES:(k + 1) * LANES] = st_re[k]
            state_ref[:, (nk + k) * LANES:(nk + k + 1) * LANES] = st_im[k]

    def stage_c(i, h, s_ref):
        s = jnp.concatenate([s_ref[k] for k in range(2 * nk)], axis=1).astype(bf16)
        yv = _dot(s, cmat_ref[direction(i)])
        rows = half_rows(i, h)
        for b in range(nb):
            y_ref[b, rows, :] = yv[b * pitch:b * pitch + ts, :]

    stage_b(2 * g, 0, bu0_ref)
    stage_c(2 * g - 2, 0, s0_ref)
    stage_scan(2 * g - 1, bu1_ref, s1_ref)
    stage_b(2 * g + 1, 1, bu1_ref)
    stage_scan(2 * g, bu0_ref, s0_ref)
    stage_c(2 * g - 1, 1, s1_ref)


def _ssm(layer, u, bmat, cmat, a_re, a_im, n_lat):
    B, NT, W = u.shape
    ts = SCAN_STEPS
    n_chunks = NT // ts
    n_blocks = n_chunks // 2
    nx = n_lat // (2 * ts)
    nc = n_blocks - nx

    def block(step):
        step = jnp.clip(step, 0, 2 * n_blocks - 1)
        d = step // n_blocks
        j = step % n_blocks
        fwd = jnp.where(j < nc, nx + j, j - nc)
        back = jnp.where(j < nc, n_blocks - 1 - j, nx - 1 - (j - nc))
        return d, jnp.where(d == 0, fwd, back)

    rows = B * SCAN_PITCH
    slab = pltpu.VMEM((2 * SSM_LANES // LANES, rows, LANES), f32)
    return pl.pallas_call(
        functools.partial(_ssm_kernel, n_chunks=n_chunks),
        out_shape=jax.ShapeDtypeStruct((2, B, NT, W), f32),
        grid=(2 * n_blocks + 1,),
        in_specs=[
            pl.BlockSpec((B, 2 * ts, W), lambda g: (0, block(g)[1], 0)),
            _layer_spec((2, W, 2 * SSM_LANES), layer),
            _layer_spec((2, 2 * SSM_LANES, W), layer),
            _layer_spec((2, B, SSM_LANES), layer),
            _layer_spec((2, B, SSM_LANES), layer),
        ],
        out_specs=pl.BlockSpec((None, B, 2 * ts, W),
                               lambda g: (block(g - 1)[0], 0, block(g - 1)[1], 0)),
        scratch_shapes=[pltpu.VMEM((rows, W), f32), slab, slab, slab, slab,
                        pltpu.VMEM((B, 2 * SSM_LANES), f32)],
        compiler_params=_params("arbitrary"),
        name="ssm_scan",
    )(u, bmat, cmat, a_re, a_im)


def _attend_group(q_pairs, keys, vals, masks, sinks):
    nq = q_pairs[0].shape[0]
    lane = lax.broadcasted_iota(jnp.int32, (nq, LANES), 1)
    low = lane < HEAD_DIM
    zero = jnp.zeros((nq, LANES), bf16)
    lhs = jnp.concatenate([jnp.where(keep, qp, zero) for qp in q_pairs for keep in (low, ~low)],
                          axis=0)
    scores = [_dot_nt(lhs, kk) for kk in keys]
    scores = [s if mk is None else s + mk for s, mk in zip(scores, masks)]
    row = lax.broadcasted_iota(jnp.int32, (4 * nq, 1), 0)
    sink = jnp.where(row < 2 * nq, jnp.where(row < nq, sinks[0], sinks[1]),
                     jnp.where(row < 3 * nq, sinks[2], sinks[3])) * LOG2_E
    cols = [s[:, c:c + LANES] for s in scores for c in range(0, s.shape[1], LANES)]
    m = jnp.maximum(sink, jnp.max(functools.reduce(jnp.maximum, cols), axis=-1, keepdims=True))
    out = jnp.zeros((4 * nq, LANES), f32)
    for s, vv in zip(scores, vals):
        out = out + _dot(jnp.exp2(s - m).astype(bf16), vv)
    out = out / (out[:, HEAD_DIM:HEAD_DIM + 1] + jnp.exp2(sink - m))
    blocks = [out[h * nq:(h + 1) * nq] for h in range(4)]
    return [jnp.where(low, blocks[2 * p], pltpu.roll(blocks[2 * p + 1], HEAD_DIM, 1))
            for p in range(2)]


def _attn_kernel(sink_ref, q_ref, k2_ref, v2_ref, o_ref, *, n_lat):
    i = pl.program_id(1)
    nq = q_ref.shape[0]
    n_ctx = k2_ref.shape[0] - n_lat
    n_lat_tiles = n_lat // nq
    span = 3 * WINDOW

    def run(keys_rows, masks):
        for hk in range(N_KV_HEADS):
            keys = [k2_ref[r, hk * LANES:(hk + 1) * LANES] for r in keys_rows]
            vals = [v2_ref[r, hk * LANES:(hk + 1) * LANES] for r in keys_rows]
            c0 = hk * GROUP * HEAD_DIM
            outs = _attend_group([q_ref[:, c0:c0 + LANES], q_ref[:, c0 + LANES:c0 + 2 * LANES]],
                                 keys, vals, masks, [sink_ref[hk * GROUP + h] for h in range(GROUP)])
            for p, o in enumerate(outs):
                o_ref[:, c0 + p * LANES:c0 + (p + 1) * LANES] = o.astype(bf16)

    ctx_rows = pl.ds(n_lat, n_ctx)

    @pl.when(i >= n_lat_tiles)
    def _():
        run([ctx_rows], [None])

    @pl.when(i < n_lat_tiles)
    def _():
        q0 = i * nq
        start = jnp.clip(q0 - WINDOW, 0, n_lat - span)
        q_pos = q0 + lax.broadcasted_iota(jnp.int32, (GROUP * nq, span), 0) % nq
        k_pos = start + lax.broadcasted_iota(jnp.int32, (GROUP * nq, span), 1)
        bias = jnp.where(jnp.abs(k_pos - q_pos) <= WINDOW, 0.0, -jnp.inf)
        win_rows = pl.ds(pl.multiple_of(start, WINDOW), span)
        run([win_rows, ctx_rows], [bias, None])


def _attention(layer, sink, q, k2, v2, n_lat):
    B, NT, _ = q.shape
    return pl.pallas_call(
        functools.partial(_attn_kernel, n_lat=n_lat),
        out_shape=jax.ShapeDtypeStruct((B, NT, ATTN_WIDTH), bf16),
        grid=(B, NT // Q_TILE),
        in_specs=[
            pl.BlockSpec(memory_space=pltpu.SMEM),
            pl.BlockSpec((None, Q_TILE, ATTN_WIDTH), lambda b, i: (b, i, 0)),
            pl.BlockSpec((None, NT, 2 * KV_WIDTH), lambda b, i: (b, 0, 0)),
            pl.BlockSpec((None, NT, 2 * KV_WIDTH), lambda b, i: (b, 0, 0)),
        ],
        out_specs=pl.BlockSpec((None, Q_TILE, ATTN_WIDTH), lambda b, i: (b, i, 0)),
        compiler_params=_params("arbitrary", "arbitrary"),
        name="attention",
    )(sink[layer], q, k2, v2)


def _outffn_kernel(x_ref, mod_ref, po_ref, u_ref, y0_ref, y1_ref, ao_ref, d_ref, gw_ref, gb_ref,
                   wo_ref, g_ref, wg_ref, wu_ref, wd_ref, o_ref):
    o1 = POOL_WIDTH
    o2 = POOL_WIDTH + SSM_WIDTH
    d_ff = wg_ref.shape[1]

    def mixer(r):
        y = d_ref[...] * u_ref[r, :] + y0_ref[r, :] + y1_ref[r, :]
        ge = _gelu_tanh(y)
        so = ge * _sigmoid(_dot(ge.astype(bf16), gw_ref[...]) + gb_ref[...])
        mix = (_dot(po_ref[r, :], wo_ref[0:o1, :]) + _dot(so.astype(bf16), wo_ref[o1:o2, :])
               + _dot(ao_ref[r, :], wo_ref[o2:, :]))
        x1 = x_ref[r, :] + mod_ref[2:3, :] * mix
        return x1, _rmsnorm_mod(x1, g_ref[...], mod_ref[4:5, :], mod_ref[3:4, :]).astype(bf16)

    def ffn(r, x1, h):
        acc = jnp.zeros_like(x1)
        for c0 in range(0, d_ff, FF_CHUNK):
            c1 = min(c0 + FF_CHUNK, d_ff)
            gate = _dot(h, wg_ref[:, c0:c1])
            up = _dot(h, wu_ref[:, c0:c1])
            acc = acc + _dot((_silu(gate) * up).astype(bf16), wd_ref[c0:c1, :])
        o_ref[r, :] = x1 + mod_ref[5:6, :] * acc

    _chained(x_ref.shape[0], mixer, ffn)


def _outffn(layer, src, mod, mod_row, po, u, y, ao, d_skip, glu_w, glu_b, w_out, g, w_gate, w_up,
            w_down, row0, block):
    B, n_src, D = src.shape
    j0 = row0 // block
    d_ff = w_gate.shape[2]
    tok = lambda w: pl.BlockSpec((None, block, w), lambda b, j: (b, j + j0, 0))
    ydir = lambda d: pl.BlockSpec((None, None, block, SSM_WIDTH), lambda b, j: (d, b, j + j0, 0))
    return pl.pallas_call(
        _outffn_kernel,
        out_shape=jax.ShapeDtypeStruct((B, n_src, D), f32),
        grid=(B, n_src // block),
        in_specs=[
            pl.BlockSpec((None, block, D), lambda b, j: (b, j, 0)),
            pl.BlockSpec((None, None, N_MOD, D), lambda b, j: (layer, mod_row(b), 0, 0)),
            tok(POOL_WIDTH),
            tok(SSM_WIDTH),
            ydir(0),
            ydir(1),
            tok(ATTN_WIDTH),
            _layer_spec((1, SSM_WIDTH), layer),
            _layer_spec((SSM_WIDTH, SSM_WIDTH), layer),
            _layer_spec((1, SSM_WIDTH), layer),
            _layer_spec((D, D), layer),
            _layer_spec((1, D), layer),
            _layer_spec((D, d_ff), layer),
            _layer_spec((D, d_ff), layer),
            _layer_spec((d_ff, D), layer),
        ],
        out_specs=pl.BlockSpec((None, block, D), lambda b, j: (b, j, 0)),
        compiler_params=_params("arbitrary", "arbitrary"),
        name="outproj_ffn",
    )(src, mod, po, u, y, y, ao, d_skip, glu_w, glu_b, w_out, g, w_gate, w_up, w_down)


def _rope_tables(n_lat, n_ctx):
    pos = jnp.arange(n_lat)
    row = (pos // GRID_W).astype(f32)
    col = (pos % GRID_W).astype(f32)
    inv = jnp.power(ROPE_BASE, -jnp.arange(ROPE_FREQS, dtype=f32) / ROPE_FREQS)
    lane = jnp.arange(LANES)
    dim = lane % HEAD_DIM
    freq = inv[dim % ROPE_FREQS]
    ang = jnp.where((dim // (2 * ROPE_FREQS)) == 0, row[:, None], col[:, None]) * freq[None, :]
    sign = jnp.where((dim % (2 * ROPE_FREQS)) < ROPE_FREQS, -1.0, 1.0).astype(f32)
    cos = jnp.concatenate([jnp.cos(ang), jnp.ones((n_ctx, LANES), f32)], axis=0)
    sin = jnp.concatenate([jnp.sin(ang) * sign, jnp.zeros((n_ctx, LANES), f32)], axis=0)
    return cos, sin


def _block_diag(blocks):
    *lead, g, r, c = blocks.shape
    eye = jnp.eye(g, dtype=blocks.dtype)
    return jnp.einsum('...grc,gh->...grhc', blocks, eye).reshape(*lead, g * r, g * c)


def _ssm_matrices(a_re, a_im, log_dt, b_re, b_im, c_re, c_im, n_batch):
    lr = jnp.minimum(a_re.astype(f32), -1e-4)
    li = a_im.astype(f32)
    dt = jnp.exp(log_dt.astype(f32))[..., None]
    mag = jnp.exp(lr * dt)
    ar = mag * jnp.cos(li * dt)
    ai = mag * jnp.sin(li * dt)
    den = lr * lr + li * li
    qr = ((ar - 1.0) * lr + ai * li) / den
    qi = (ai * lr - (ar - 1.0) * li) / den
    bbr = qr[..., None] * b_re - qi[..., None] * b_im
    bbi = qr[..., None] * b_im + qi[..., None] * b_re
    bmat = jnp.concatenate([_block_diag(jnp.swapaxes(bbr, -1, -2)),
                            _block_diag(jnp.swapaxes(bbi, -1, -2))], axis=-1)
    cmat = jnp.concatenate([_block_diag(jnp.swapaxes(c_re.astype(f32), -1, -2)),
                            -_block_diag(jnp.swapaxes(c_im.astype(f32), -1, -2))], axis=-2)
    shape = a_re.shape[:2] + (n_batch, SSM_LANES)
    a_re_b = jnp.broadcast_to(ar.reshape(a_re.shape[:2] + (1, SSM_LANES)), shape)
    a_im_b = jnp.broadcast_to(ai.reshape(a_re.shape[:2] + (1, SSM_LANES)), shape)
    return bmat.astype(bf16), cmat.astype(bf16), a_re_b, a_im_b


def kernel(x, c, ctx, c_ctx, w_mod, b_mod, norm_mix, norm_ffn, w_in, w_out, pool_w, pool_scale, ssm_a_re, ssm_a_im, ssm_log_dt, ssm_b_re, ssm_b_im, ssm_c_re, ssm_c_im, ssm_d, ssm_glu_w, ssm_glu_b, q_norm, k_norm, attn_sink, ffn_w_gate, ffn_w_up, ffn_w_down):
    B, L, D = x.shape
    Lc = ctx.shape[1]
    NT = L + Lc
    depth = w_mod.shape[0]
    assert Lc == CHAIN_ROWS and L % LATENT_BLOCK == 0 and L % GRID_W == 0 and L >= 3 * WINDOW
    assert B == SUBLANES and w_in.shape[2] == POOL_WIDTH + SSM_WIDTH + ATTN_WIDTH + 2 * KV_WIDTH
    assert w_mod.shape[2] == N_MOD * D and Lc % (2 * SCAN_STEPS) == 0 and L % (2 * SCAN_STEPS) == 0

    c16 = jnp.concatenate([c, c_ctx[None, :], jnp.zeros((16 - B - 1, D), f32)], axis=0)
    mod = _modulation(c16, w_mod, b_mod).reshape(depth, 16, N_MOD, D)
    lat_row = lambda b: b
    ctx_row = lambda b: B

    cos_t, sin_t = _rope_tables(L, Lc)
    head = jnp.arange(ATTN_WIDTH) // HEAD_DIM
    ones = jnp.where(head[:, None] == head[None, :], 1.0 / HEAD_DIM, 0.0).astype(bf16)
    qn = (jnp.tile(q_norm, (1, N_HEADS)) * (HEAD_DIM ** -0.5 * LOG2_E)).reshape(depth, 1, ATTN_WIDTH)
    kn = jnp.tile(k_norm, (1, N_KV_HEADS)).reshape(depth, 1, KV_WIDTH)
    bmat, cmat, a_re, a_im = _ssm_matrices(ssm_a_re, ssm_a_im, ssm_log_dt, ssm_b_re, ssm_b_im,
                                           ssm_c_re, ssm_c_im, B)
    pool_bd = _block_diag(pool_w).astype(bf16)
    w_in_b, w_out_b, glu_w_b = w_in.astype(bf16), w_out.astype(bf16), ssm_glu_w.astype(bf16)
    w_gate_b, w_up_b, w_down_b = ffn_w_gate.astype(bf16), ffn_w_up.astype(bf16), ffn_w_down.astype(bf16)
    row = lambda a: a.reshape(depth, 1, a.shape[-1])

    for l in range(depth):
        proj_args = (row(norm_mix), w_in_b, cos_t, sin_t, qn, kn, ones, NT)
        outs = _inproj(l, x, mod, lat_row, *proj_args, 0, LATENT_BLOCK)
        pool_u, u, q, k2, v2 = _inproj(l, ctx, mod, ctx_row, *proj_args, L, CHAIN_ROWS, prev=outs)
        po = _pool(l, pool_u, pool_bd, row(pool_scale), L)
        y = _ssm(l, u, bmat, cmat, a_re, a_im, L)
        ao = _attention(l, attn_sink, q, k2, v2, L)
        ffn_args = (po, u, y, ao, row(ssm_d), glu_w_b, row(ssm_glu_b), w_out_b, row(norm_ffn),
                    w_gate_b, w_up_b, w_down_b)
        x_new = _outffn(l, x, mod, lat_row, *ffn_args, 0, LATENT_BLOCK)
        if l < depth - 1:
            ctx = _outffn(l, ctx, mod, ctx_row, *ffn_args, L, CHAIN_ROWS)
        x = x_new
    return x
```

```python
import functools
import math

import jax
import jax.numpy as jnp
from jax import lax
from jax.experimental import pallas as pl
from jax.experimental.pallas import tpu as pltpu

f32 = jnp.float32
bf16 = jnp.bfloat16

GRID_W = 64
POOL_WINDOWS = (2, 4, 8, 16)
POOL_GROUP = 64
POOL_WIDTH = POOL_GROUP * len(POOL_WINDOWS)
SSM_GROUP_CH = 16
SSM_GROUPS = 16
SSM_WIDTH = SSM_GROUP_CH * SSM_GROUPS
SSM_STATE = 64
SSM_LANES = SSM_GROUPS * SSM_STATE
HEAD_DIM = 64
N_HEADS = 8
N_KV_HEADS = 2
GROUP = N_HEADS // N_KV_HEADS
ATTN_WIDTH = N_HEADS * HEAD_DIM
KV_WIDTH = N_KV_HEADS * HEAD_DIM
WINDOW = 128
ROPE_BASE = 10000.0
ROPE_FREQS = HEAD_DIM // 4
N_MOD = 6
EPS = 1e-6
LOG2_E = math.log2(math.e)

LANES = 128
SUBLANES = 8
MXU_TILE = 256
CHAIN_ROWS = 256
LATENT_BLOCK = 4 * CHAIN_ROWS
Q_TILE = 128
SCAN_STEPS = 64
SCAN_PITCH = SCAN_STEPS + 4
FF_CHUNK = 6 * MXU_TILE
VMEM_LIMIT_BYTES = 56 * 1024 * 1024


def _sigmoid(x):
    return 1.0 / (1.0 + jnp.exp(-x))


def _silu(x):
    return x * _sigmoid(x)


def _gelu_tanh(x):
    return 0.5 * x * (1.0 + jnp.tanh(math.sqrt(2.0 / math.pi) * (x + 0.044715 * (x * x * x))))


def _dot(a, b):
    return jnp.dot(a, b, preferred_element_type=f32)


def _dot_nt(a, b):
    return lax.dot_general(a, b, (((1,), (1,)), ((), ())), preferred_element_type=f32)


def _const_spec(shape):
    nd = len(shape)
    return pl.BlockSpec(shape, lambda *_: (0,) * nd, pipeline_mode=pl.Buffered(1))


def _layer_spec(shape, layer):
    nd = len(shape)
    return pl.BlockSpec((None,) + tuple(shape), lambda *_: (layer,) + (0,) * nd,
                        pipeline_mode=pl.Buffered(1))


def _params(*sem):
    return pltpu.CompilerParams(dimension_semantics=sem, vmem_limit_bytes=VMEM_LIMIT_BYTES)


def _chained(n_rows, head, tail):
    rows = [slice(r, r + CHAIN_ROWS) for r in range(0, n_rows, CHAIN_ROWS)]
    nxt = head(rows[0])
    for i, r in enumerate(rows):
        cur = nxt
        if i + 1 < len(rows):
            nxt = head(rows[i + 1])
        tail(r, *cur)


def _mod_kernel(c_ref, w_ref, b_ref, o_ref):
    a = _silu(c_ref[...]).astype(bf16)
    o_ref[...] = _dot(a, w_ref[...].astype(bf16)) + b_ref[...]


def _modulation(c16, w_mod, b_mod):
    depth, d, n = w_mod.shape
    tn = 1536
    return pl.pallas_call(
        _mod_kernel,
        out_shape=jax.ShapeDtypeStruct((depth, 16, n), f32),
        grid=(depth, n // tn),
        in_specs=[
            pl.BlockSpec((16, d), lambda l, j: (0, 0)),
            pl.BlockSpec((None, d, tn), lambda l, j: (l, 0, j)),
            pl.BlockSpec((None, 1, tn), lambda l, j: (l, 0, j)),
        ],
        out_specs=pl.BlockSpec((None, 16, tn), lambda l, j: (l, 0, j)),
        compiler_params=_params("arbitrary", "arbitrary"),
        name="modulation",
    )(c16, w_mod, b_mod.reshape(depth, 1, n))


def _rmsnorm_mod(x, g, scale, shift):
    y = x * lax.rsqrt(jnp.mean(x * x, axis=-1, keepdims=True) + EPS) * g
    return y * (1.0 + scale) + shift


def _rope(x, cos, sin_signed):
    lane = lax.broadcasted_iota(jnp.int32, x.shape, 1)
    first = (lane % 32) < 16
    partner = jnp.where(first, pltpu.roll(x, LANES - 16, 1), pltpu.roll(x, 16, 1))
    return x * cos + partner * sin_signed


def _inproj_kernel(x_ref, mod_ref, g_ref, w_ref, cos_ref, sin_ref, qn_ref, kn_ref, ones_ref,
                   *refs):
    pool_ref, ssm_ref, q_ref, k2_ref, v2_ref = refs[-5:]

    def norm(r):
        return (_rmsnorm_mod(x_ref[r, :], g_ref[...], mod_ref[1:2, :], mod_ref[0:1, :]).astype(bf16),)

    def project(r, h):
        proj = _dot(h, w_ref[...])
        pool_ref[r, :] = proj[:, 0:POOL_WIDTH]
        ssm_ref[r, :] = proj[:, POOL_WIDTH:POOL_WIDTH + SSM_WIDTH]
        o = POOL_WIDTH + SSM_WIDTH
        q = proj[:, o:o + ATTN_WIDTH]
        k = proj[:, o + ATTN_WIDTH:o + ATTN_WIDTH + KV_WIDTH]
        v = proj[:, o + ATTN_WIDTH + KV_WIDTH:o + ATTN_WIDTH + 2 * KV_WIDTH]
        cos = cos_ref[r, :]
        sin = sin_ref[r, :]
        q_ms = _dot((q * q).astype(bf16), ones_ref[...])
        qn = q * lax.rsqrt(q_ms + EPS) * qn_ref[...]
        for cblk in range(ATTN_WIDTH // LANES):
            sl = slice(cblk * LANES, (cblk + 1) * LANES)
            q_ref[r, sl] = _rope(qn[:, sl], cos, sin).astype(bf16)
        k_ms = _dot((k * k).astype(bf16), ones_ref[0:KV_WIDTH, 0:KV_WIDTH])
        kr = _rope(k * lax.rsqrt(k_ms + EPS) * kn_ref[...], cos, sin)
        lane = lax.broadcasted_iota(jnp.int32, kr.shape, 1)
        left = lane < HEAD_DIM
        ksw = pltpu.roll(kr, HEAD_DIM, 1)
        k2_ref[r, 0:LANES] = jnp.where(left, kr, ksw).astype(bf16)
        k2_ref[r, LANES:2 * LANES] = jnp.where(left, ksw, kr).astype(bf16)
        vsw = pltpu.roll(v, HEAD_DIM, 1)
        ones_col = jnp.where(lane == HEAD_DIM, 1.0, 0.0)
        v2_ref[r, 0:LANES] = jnp.where(left, v, ones_col).astype(bf16)
        v2_ref[r, LANES:2 * LANES] = jnp.where(left, vsw, ones_col).astype(bf16)

    _chained(x_ref.shape[0], norm, project)


def _inproj(layer, src, mod, mod_row, g, w_in, cos_t, sin_t, qn, kn, ones, n_total, row0, block,
            prev=None):
    B, n_src, D = src.shape
    j0 = row0 // block
    win = w_in.shape[2]
    widths = (POOL_WIDTH, SSM_WIDTH, ATTN_WIDTH, 2 * KV_WIDTH, 2 * KV_WIDTH)
    dtypes = (f32, f32, bf16, bf16, bf16)
    in_specs = [
        pl.BlockSpec((None, block, D), lambda b, j: (b, j, 0)),
        pl.BlockSpec((None, None, N_MOD, D), lambda b, j: (layer, mod_row(b), 0, 0)),
        _layer_spec((1, D), layer),
        _layer_spec((D, win), layer),
        pl.BlockSpec((block, LANES), lambda b, j: (j + j0, 0)),
        pl.BlockSpec((block, LANES), lambda b, j: (j + j0, 0)),
        _layer_spec((1, ATTN_WIDTH), layer),
        _layer_spec((1, KV_WIDTH), layer),
        _const_spec((ATTN_WIDTH, ATTN_WIDTH)),
    ]
    args = [src, mod, g, w_in, cos_t, sin_t, qn, kn, ones]
    aliases = {}
    if prev is not None:
        aliases = {len(args) + i: i for i in range(len(prev))}
        in_specs += [pl.BlockSpec(memory_space=pl.ANY)] * len(prev)
        args += list(prev)
    return pl.pallas_call(
        _inproj_kernel,
        out_shape=tuple(jax.ShapeDtypeStruct((B, n_total, w), dt) for w, dt in zip(widths, dtypes)),
        grid=(B, n_src // block),
        in_specs=in_specs,
        out_specs=tuple(pl.BlockSpec((None, block, w), lambda b, j: (b, j + j0, 0)) for w in widths),
        input_output_aliases=aliases,
        compiler_params=_params("arbitrary", "arbitrary"),
        name="inproj",
    )(*args)


def _shift_rows(x, s, row):
    n = x.shape[0]
    if s > 0:
        return jnp.where(row >= s, pltpu.roll(x, s, 0), 0.0)
    return jnp.where(row < n + s, pltpu.roll(x, n + s, 0), 0.0)


def _window_sums(u, half_a, half_b, lane_split):
    row = lax.broadcasted_iota(jnp.int32, u.shape, 0)
    lane = lax.broadcasted_iota(jnp.int32, u.shape, 1)
    bwd = u
    fwd = u
    k = 1
    out_a = None
    while True:
        if k == half_a:
            out_a = _shift_rows(bwd, 1, row) + fwd
        if k == half_b:
            out_b = _shift_rows(bwd, 1, row) + fwd
            break
        bwd = bwd + _shift_rows(bwd, k, row)
        fwd = fwd + _shift_rows(fwd, -k, row)
        k *= 2
    n = u.shape[0]
    half = jnp.where(lane < lane_split, half_a, half_b)
    cnt = jnp.minimum(row + half, n) - jnp.maximum(row - half, 0)
    return jnp.where(lane < lane_split, out_a, out_b) / cnt.astype(f32)


def _pool_segment(u_ref, w_ref, scale_ref, o_ref, r0, n):
    parts = []
    for blk in range(POOL_WIDTH // LANES):
        u = u_ref[r0:r0 + n, blk * LANES:(blk + 1) * LANES]
        wa, wb = POOL_WINDOWS[2 * blk], POOL_WINDOWS[2 * blk + 1]
        mean = _window_sums(u, wa // 2, wb // 2, POOL_GROUP)
        parts.append((mean - u).astype(bf16))
    d = jnp.concatenate(parts, axis=1)
    o_ref[r0:r0 + n, :] = (_dot(d, w_ref[...]) * scale_ref[...]).astype(bf16)


def _pool_kernel(u_ref, w_ref, scale_ref, o_ref, *, n_lat):
    nt = u_ref.shape[0]
    _pool_segment(u_ref, w_ref, scale_ref, o_ref, 0, n_lat)
    _pool_segment(u_ref, w_ref, scale_ref, o_ref, n_lat, nt - n_lat)


def _pool(layer, pool_u, w_bd, scale, n_lat):
    B, NT, W = pool_u.shape
    return pl.pallas_call(
        functools.partial(_pool_kernel, n_lat=n_lat),
        out_shape=jax.ShapeDtypeStruct((B, NT, W), bf16),
        grid=(B,),
        in_specs=[
            pl.BlockSpec((None, NT, W), lambda b: (b, 0, 0)),
            _layer_spec((W, W), layer),
            _layer_spec((1, W), layer),
        ],
        out_specs=pl.BlockSpec((None, NT, W), lambda b: (b, 0, 0)),
        compiler_params=_params("arbitrary"),
        name="pool",
    )(pool_u, w_bd, scale)


def _ssm_kernel(u_ref, bmat_ref, cmat_ref, are_ref, aim_ref, y_ref, upad_ref, state_ref,
                *bufs, n_chunks):
    g = pl.program_id(0)
    nb = u_ref.shape[0]
    ts, pitch = SCAN_STEPS, SCAN_PITCH
    n_seq = 2 * n_chunks
    nk = SSM_LANES // LANES
    bu_even, bu_odd, s_even, s_odd = bufs[0:2], bufs[2:4], bufs[4:6], bufs[6:8]

    @pl.when(g == 0)
    def _():
        for r in (upad_ref, state_ref) + tuple(bufs):
            r[...] = jnp.zeros_like(r)

    def direction(i):
        return jnp.clip(i, 0, n_seq - 1) // n_chunks

    def half_rows(i, h):
        off = jnp.where(direction(i) == 0, h, 1 - h) * ts
        return pl.ds(pl.multiple_of(off, ts), ts)

    def stage_b(i, h, bu_ref):
        rows = half_rows(i, h)
        for b in range(nb):
            upad_ref[b * pitch:b * pitch + ts, :] = u_ref[b, rows, :]
        res = _dot(upad_ref[...].astype(bf16), bmat_ref[direction(i)])
        for k in range(2 * nk):
            bu_ref[k] = res[:, k * LANES:(k + 1) * LANES]

    def stage_scan(i, bu_ref, s_ref):
        ic = jnp.clip(i, 0, n_seq - 1)
        d = ic // n_chunks
        keep = jnp.where((ic % n_chunks) == 0, 0.0, 1.0)
        a_re = are_ref[d]
        a_im = aim_ref[d]
        st_re = [state_ref[:, k * LANES:(k + 1) * LANES] * keep for k in range(nk)]
        st_im = [state_ref[:, (nk + k) * LANES:(nk + k + 1) * LANES] * keep for k in range(nk)]
        for t in range(ts):
            tt = jnp.where(d == 0, t, ts - 1 - t)
            rows = pl.ds(tt, nb, stride=pitch)
            for k in range(nk):
                ar = a_re[:, k * LANES:(k + 1) * LANES]
                ai = a_im[:, k * LANES:(k + 1) * LANES]
                n_re = ar * st_re[k] - ai * st_im[k] + bu_ref[k, rows, :]
                n_im = ar * st_im[k] + ai * st_re[k] + bu_ref[nk + k, rows, :]
                s_ref[k, rows, :] = n_re
                s_ref[nk + k, rows, :] = n_im
                st_re[k], st_im[k] = n_re, n_im
        for k in range(nk):
            state_ref[:, k * LANES:(k + 1) * LANES] = st_re[k]
            state_ref[:, (nk + k) * LANES:(nk + k + 1) * LANES] = st_im[k]

    def stage_c(i, h, s_ref):
        s = jnp.concatenate([s_ref[k] for k in range(2 * nk)], axis=1).astype(bf16)
        yv = _dot(s, cmat_ref[direction(i)])
        rows = half_rows(i, h)
        for b in range(nb):
            y_ref[b, rows, :] = yv[b * pitch:b * pitch + ts, :]

    def step(bu_write, bu_read, s_write, s_read):
        for h in range(2):
            stage_scan(2 * g - 2 + h, bu_read[h], s_write[h])
            stage_b(2 * g + h, h, bu_write[h])
            stage_c(2 * g - 4 + h, h, s_read[h])

    @pl.when(g % 2 == 0)
    def _():
        step(bu_even, bu_odd, s_even, s_odd)

    @pl.when(g % 2 == 1)
    def _():
        step(bu_odd, bu_even, s_odd, s_even)


def _ssm(layer, u, bmat, cmat, a_re, a_im, n_lat):
    B, NT, W = u.shape
    ts = SCAN_STEPS
    n_chunks = NT // ts
    n_blocks = n_chunks // 2
    nx = n_lat // (2 * ts)
    nc = n_blocks - nx

    def block(step):
        step = jnp.clip(step, 0, 2 * n_blocks - 1)
        d = step // n_blocks
        j = step % n_blocks
        fwd = jnp.where(j < nc, nx + j, j - nc)
        back = jnp.where(j < nc, n_blocks - 1 - j, nx - 1 - (j - nc))
        return d, jnp.where(d == 0, fwd, back)

    rows = B * SCAN_PITCH
    slab = pltpu.VMEM((2 * SSM_LANES // LANES, rows, LANES), f32)
    return pl.pallas_call(
        functools.partial(_ssm_kernel, n_chunks=n_chunks),
        out_shape=jax.ShapeDtypeStruct((2, B, NT, W), f32),
        grid=(2 * n_blocks + 2,),
        in_specs=[
            pl.BlockSpec((B, 2 * ts, W), lambda g: (0, block(g)[1], 0)),
            _layer_spec((2, W, 2 * SSM_LANES), layer),
            _layer_spec((2, 2 * SSM_LANES, W), layer),
            _layer_spec((2, B, SSM_LANES), layer),
            _layer_spec((2, B, SSM_LANES), layer),
        ],
        out_specs=pl.BlockSpec((None, B, 2 * ts, W),
                               lambda g: (block(g - 2)[0], 0, block(g - 2)[1], 0)),
        scratch_shapes=[pltpu.VMEM((rows, W), f32), pltpu.VMEM((B, 2 * SSM_LANES), f32)]
        + [slab] * 8,
        compiler_params=_params("arbitrary"),
        name="ssm_scan",
    )(u, bmat, cmat, a_re, a_im)


def _attend_group(q_pairs, keys, vals, masks, sinks):
    nq = q_pairs[0].shape[0]
    lane = lax.broadcasted_iota(jnp.int32, (nq, LANES), 1)
    low = lane < HEAD_DIM
    zero = jnp.zeros((nq, LANES), bf16)
    lhs = jnp.concatenate([jnp.where(keep, qp, zero) for qp in q_pairs for keep in (low, ~low)],
                          axis=0)
    scores = [_dot_nt(lhs, kk) for kk in keys]
    scores = [s if mk is None else s + mk for s, mk in zip(scores, masks)]
    row = lax.broadcasted_iota(jnp.int32, (4 * nq, 1), 0)
    sink = jnp.where(row < 2 * nq, jnp.where(row < nq, sinks[0], sinks[1]),
                     jnp.where(row < 3 * nq, sinks[2], sinks[3])) * LOG2_E
    cols = [s[:, c:c + LANES] for s in scores for c in range(0, s.shape[1], LANES)]
    m = jnp.maximum(sink, jnp.max(functools.reduce(jnp.maximum, cols), axis=-1, keepdims=True))
    out = jnp.zeros((4 * nq, LANES), f32)
    for s, vv in zip(scores, vals):
        out = out + _dot(jnp.exp2(s - m).astype(bf16), vv)
    out = out / (out[:, HEAD_DIM:HEAD_DIM + 1] + jnp.exp2(sink - m))
    blocks = [out[h * nq:(h + 1) * nq] for h in range(4)]
    return [jnp.where(low, blocks[2 * p], pltpu.roll(blocks[2 * p + 1], HEAD_DIM, 1))
            for p in range(2)]


def _attn_kernel(sink_ref, q_ref, k2_ref, v2_ref, o_ref, *, n_lat):
    i = pl.program_id(1)
    nq = q_ref.shape[0]
    n_ctx = k2_ref.shape[0] - n_lat
    n_lat_tiles = n_lat // nq
    span = 3 * WINDOW

    def run(keys_rows, masks):
        for hk in range(N_KV_HEADS):
            keys = [k2_ref[r, hk * LANES:(hk + 1) * LANES] for r in keys_rows]
            vals = [v2_ref[r, hk * LANES:(hk + 1) * LANES] for r in keys_rows]
            c0 = hk * GROUP * HEAD_DIM
            outs = _attend_group([q_ref[:, c0:c0 + LANES], q_ref[:, c0 + LANES:c0 + 2 * LANES]],
                                 keys, vals, masks, [sink_ref[hk * GROUP + h] for h in range(GROUP)])
            for p, o in enumerate(outs):
                o_ref[:, c0 + p * LANES:c0 + (p + 1) * LANES] = o.astype(bf16)

    ctx_rows = pl.ds(n_lat, n_ctx)

    @pl.when(i >= n_lat_tiles)
    def _():
        run([ctx_rows], [None])

    @pl.when(i < n_lat_tiles)
    def _():
        q0 = i * nq
        start = jnp.clip(q0 - WINDOW, 0, n_lat - span)
        q_pos = q0 + lax.broadcasted_iota(jnp.int32, (GROUP * nq, span), 0) % nq
        k_pos = start + lax.broadcasted_iota(jnp.int32, (GROUP * nq, span), 1)
        bias = jnp.where(jnp.abs(k_pos - q_pos) <= WINDOW, 0.0, -jnp.inf)
        win_rows = pl.ds(pl.multiple_of(start, WINDOW), span)
        run([win_rows, ctx_rows], [bias, None])


def _attention(layer, sink, q, k2, v2, n_lat):
    B, NT, _ = q.shape
    return pl.pallas_call(
        functools.partial(_attn_kernel, n_lat=n_lat),
        out_shape=jax.ShapeDtypeStruct((B, NT, ATTN_WIDTH), bf16),
        grid=(B, NT // Q_TILE),
        in_specs=[
            pl.BlockSpec(memory_space=pltpu.SMEM),
            pl.BlockSpec((None, Q_TILE, ATTN_WIDTH), lambda b, i: (b, i, 0)),
            pl.BlockSpec((None, NT, 2 * KV_WIDTH), lambda b, i: (b, 0, 0)),
            pl.BlockSpec((None, NT, 2 * KV_WIDTH), lambda b, i: (b, 0, 0)),
        ],
        out_specs=pl.BlockSpec((None, Q_TILE, ATTN_WIDTH), lambda b, i: (b, i, 0)),
        compiler_params=_params("arbitrary", "arbitrary"),
        name="attention",
    )(sink[layer], q, k2, v2)


def _outffn_kernel(x_ref, mod_ref, po_ref, u_ref, y0_ref, y1_ref, ao_ref, d_ref, gw_ref, gb_ref,
                   wo_ref, g_ref, wg_ref, wu_ref, wd_ref, o_ref):
    o1 = POOL_WIDTH
    o2 = POOL_WIDTH + SSM_WIDTH
    d_ff = wg_ref.shape[1]

    def mixer(r):
        y = d_ref[...] * u_ref[r, :] + y0_ref[r, :] + y1_ref[r, :]
        ge = _gelu_tanh(y)
        so = ge * _sigmoid(_dot(ge.astype(bf16), gw_ref[...]) + gb_ref[...])
        mix = (_dot(po_ref[r, :], wo_ref[0:o1, :]) + _dot(so.astype(bf16), wo_ref[o1:o2, :])
               + _dot(ao_ref[r, :], wo_ref[o2:, :]))
        x1 = x_ref[r, :] + mod_ref[2:3, :] * mix
        return x1, _rmsnorm_mod(x1, g_ref[...], mod_ref[4:5, :], mod_ref[3:4, :]).astype(bf16)

    def ffn(r, x1, h):
        acc = jnp.zeros_like(x1)
        for c0 in range(0, d_ff, FF_CHUNK):
            c1 = min(c0 + FF_CHUNK, d_ff)
            gate = _dot(h, wg_ref[:, c0:c1])
            up = _dot(h, wu_ref[:, c0:c1])
            acc = acc + _dot((_silu(gate) * up).astype(bf16), wd_ref[c0:c1, :])
        o_ref[r, :] = x1 + mod_ref[5:6, :] * acc

    _chained(x_ref.shape[0], mixer, ffn)


def _outffn(layer, src, mod, mod_row, po, u, y, ao, d_skip, glu_w, glu_b, w_out, g, w_gate, w_up,
            w_down, row0, block):
    B, n_src, D = src.shape
    j0 = row0 // block
    d_ff = w_gate.shape[2]
    tok = lambda w: pl.BlockSpec((None, block, w), lambda b, j: (b, j + j0, 0))
    ydir = lambda d: pl.BlockSpec((None, None, block, SSM_WIDTH), lambda b, j: (d, b, j + j0, 0))
    return pl.pallas_call(
        _outffn_kernel,
        out_shape=jax.ShapeDtypeStruct((B, n_src, D), f32),
        grid=(B, n_src // block),
        in_specs=[
            pl.BlockSpec((None, block, D), lambda b, j: (b, j, 0)),
            pl.BlockSpec((None, None, N_MOD, D), lambda b, j: (layer, mod_row(b), 0, 0)),
            tok(POOL_WIDTH),
            tok(SSM_WIDTH),
            ydir(0),
            ydir(1),
            tok(ATTN_WIDTH),
            _layer_spec((1, SSM_WIDTH), layer),
            _layer_spec((SSM_WIDTH, SSM_WIDTH), layer),
            _layer_spec((1, SSM_WIDTH), layer),
            _layer_spec((D, D), layer),
            _layer_spec((1, D), layer),
            _layer_spec((D, d_ff), layer),
            _layer_spec((D, d_ff), layer),
            _layer_spec((d_ff, D), layer),
        ],
        out_specs=pl.BlockSpec((None, block, D), lambda b, j: (b, j, 0)),
        compiler_params=_params("arbitrary", "arbitrary"),
        name="outproj_ffn",
    )(src, mod, po, u, y, y, ao, d_skip, glu_w, glu_b, w_out, g, w_gate, w_up, w_down)


def _rope_tables(n_lat, n_ctx):
    pos = jnp.arange(n_lat)
    row = (pos // GRID_W).astype(f32)
    col = (pos % GRID_W).astype(f32)
    inv = jnp.power(ROPE_BASE, -jnp.arange(ROPE_FREQS, dtype=f32) / ROPE_FREQS)
    lane = jnp.arange(LANES)
    dim = lane % HEAD_DIM
    freq = inv[dim % ROPE_FREQS]
    ang = jnp.where((dim // (2 * ROPE_FREQS)) == 0, row[:, None], col[:, None]) * freq[None, :]
    sign = jnp.where((dim % (2 * ROPE_FREQS)) < ROPE_FREQS, -1.0, 1.0).astype(f32)
    cos = jnp.concatenate([jnp.cos(ang), jnp.ones((n_ctx, LANES), f32)], axis=0)
    sin = jnp.concatenate([jnp.sin(ang) * sign, jnp.zeros((n_ctx, LANES), f32)], axis=0)
    return cos, sin


def _block_diag(blocks):
    *lead, g, r, c = blocks.shape
    on_diag = (jnp.arange(g)[:, None, None, None] == jnp.arange(g)[None, None, :, None])
    spread = jnp.where(on_diag, blocks[..., :, :, None, :], jnp.zeros((), blocks.dtype))
    return spread.reshape(*lead, g * r, g * c)


def _ssm_matrices(a_re, a_im, log_dt, b_re, b_im, c_re, c_im, n_batch):
    lr = jnp.minimum(a_re.astype(f32), -1e-4)
    li = a_im.astype(f32)
    dt = jnp.exp(log_dt.astype(f32))[..., None]
    mag = jnp.exp(lr * dt)
    ar = mag * jnp.cos(li * dt)
    ai = mag * jnp.sin(li * dt)
    den = lr * lr + li * li
    qr = ((ar - 1.0) * lr + ai * li) / den
    qi = (ai * lr - (ar - 1.0) * li) / den
    bbr = qr[..., None] * b_re - qi[..., None] * b_im
    bbi = qr[..., None] * b_im + qi[..., None] * b_re
    bd = lambda a: _block_diag(jnp.swapaxes(a, -1, -2).astype(bf16))
    bmat = jnp.concatenate([bd(bbr), bd(bbi)], axis=-1)
    cmat = jnp.concatenate([bd(c_re), bd(-c_im)], axis=-2)
    shape = a_re.shape[:2] + (n_batch, SSM_LANES)
    a_re_b = jnp.broadcast_to(ar.reshape(a_re.shape[:2] + (1, SSM_LANES)), shape)
    a_im_b = jnp.broadcast_to(ai.reshape(a_re.shape[:2] + (1, SSM_LANES)), shape)
    return bmat.astype(bf16), cmat.astype(bf16), a_re_b, a_im_b


def kernel(x, c, ctx, c_ctx, w_mod, b_mod, norm_mix, norm_ffn, w_in, w_out, pool_w, pool_scale, ssm_a_re, ssm_a_im, ssm_log_dt, ssm_b_re, ssm_b_im, ssm_c_re, ssm_c_im, ssm_d, ssm_glu_w, ssm_glu_b, q_norm, k_norm, attn_sink, ffn_w_gate, ffn_w_up, ffn_w_down):
    B, L, D = x.shape
    Lc = ctx.shape[1]
    NT = L + Lc
    depth = w_mod.shape[0]
    assert Lc == CHAIN_ROWS and L % LATENT_BLOCK == 0 and L % GRID_W == 0 and L >= 3 * WINDOW
    assert B == SUBLANES and w_in.shape[2] == POOL_WIDTH + SSM_WIDTH + ATTN_WIDTH + 2 * KV_WIDTH
    assert w_mod.shape[2] == N_MOD * D and Lc % (2 * SCAN_STEPS) == 0 and L % (2 * SCAN_STEPS) == 0

    c16 = jnp.concatenate([c, c_ctx[None, :], jnp.zeros((16 - B - 1, D), f32)], axis=0)
    mod = _modulation(c16, w_mod, b_mod).reshape(depth, 16, N_MOD, D)
    lat_row = lambda b: b
    ctx_row = lambda b: B

    cos_t, sin_t = _rope_tables(L, Lc)
    head = jnp.arange(ATTN_WIDTH) // HEAD_DIM
    ones = jnp.where(head[:, None] == head[None, :], 1.0 / HEAD_DIM, 0.0).astype(bf16)
    qn = (jnp.tile(q_norm, (1, N_HEADS)) * (HEAD_DIM ** -0.5 * LOG2_E)).reshape(depth, 1, ATTN_WIDTH)
    kn = jnp.tile(k_norm, (1, N_KV_HEADS)).reshape(depth, 1, KV_WIDTH)
    bmat, cmat, a_re, a_im = _ssm_matrices(ssm_a_re, ssm_a_im, ssm_log_dt, ssm_b_re, ssm_b_im,
                                           ssm_c_re, ssm_c_im, B)
    pool_bd = _block_diag(pool_w).astype(bf16)
    w_in_b, w_out_b, glu_w_b = w_in.astype(bf16), w_out.astype(bf16), ssm_glu_w.astype(bf16)
    w_gate_b, w_up_b, w_down_b = ffn_w_gate.astype(bf16), ffn_w_up.astype(bf16), ffn_w_down.astype(bf16)
    row = lambda a: a.reshape(depth, 1, a.shape[-1])

    for l in range(depth):
        proj_args = (row(norm_mix), w_in_b, cos_t, sin_t, qn, kn, ones, NT)
        outs = _inproj(l, x, mod, lat_row, *proj_args, 0, LATENT_BLOCK)
        pool_u, u, q, k2, v2 = _inproj(l, ctx, mod, ctx_row, *proj_args, L, CHAIN_ROWS, prev=outs)
        po = _pool(l, pool_u, pool_bd, row(pool_scale), L)
        y = _ssm(l, u, bmat, cmat, a_re, a_im, L)
        ao = _attention(l, attn_sink, q, k2, v2, L)
        ffn_args = (po, u, y, ao, row(ssm_d), glu_w_b, row(ssm_glu_b), w_out_b, row(norm_ffn),
                    w_gate_b, w_up_b, w_down_b)
        x_new = _outffn(l, x, mod, lat_row, *ffn_args, 0, LATENT_BLOCK)
        if l < depth - 1:
            ctx = _outffn(l, ctx, mod, ctx_row, *ffn_args, L, CHAIN_ROWS)
        x = x_new
    return x
```

```python
import functools
import math

import jax
import jax.numpy as jnp
from jax import lax
from jax.experimental import pallas as pl
from jax.experimental.pallas import tpu as pltpu

f32 = jnp.float32
bf16 = jnp.bfloat16

GRID_W = 64
POOL_WINDOWS = (2, 4, 8, 16)
POOL_GROUP = 64
POOL_WIDTH = POOL_GROUP * len(POOL_WINDOWS)
SSM_GROUP_CH = 16
SSM_GROUPS = 16
SSM_WIDTH = SSM_GROUP_CH * SSM_GROUPS
SSM_STATE = 64
SSM_LANES = SSM_GROUPS * SSM_STATE
HEAD_DIM = 64
N_HEADS = 8
N_KV_HEADS = 2
GROUP = N_HEADS // N_KV_HEADS
ATTN_WIDTH = N_HEADS * HEAD_DIM
KV_WIDTH = N_KV_HEADS * HEAD_DIM
WINDOW = 128
ROPE_BASE = 10000.0
ROPE_FREQS = HEAD_DIM // 4
N_MOD = 6
EPS = 1e-6
LOG2_E = math.log2(math.e)

LANES = 128
SUBLANES = 8
MXU_TILE = 256
CHAIN_ROWS = 256
LATENT_BLOCK = 4 * CHAIN_ROWS
CTX_BATCHES = 4
Q_TILE = 128
Q_BLOCK = 2 * Q_TILE
SCAN_STEPS = 64
SCAN_PITCH = SCAN_STEPS + 4
FF_CHUNK = 6 * MXU_TILE
VMEM_LIMIT_BYTES = 56 * 1024 * 1024


def _sigmoid(x):
    return 1.0 / (1.0 + jnp.exp(-x))


def _silu(x):
    return x * _sigmoid(x)


def _gelu_tanh(x):
    return 0.5 * x * (1.0 + jnp.tanh(math.sqrt(2.0 / math.pi) * (x + 0.044715 * (x * x * x))))


def _dot(a, b):
    return jnp.dot(a, b, preferred_element_type=f32)


def _dot_nt(a, b):
    return lax.dot_general(a, b, (((1,), (1,)), ((), ())), preferred_element_type=f32)


def _const_spec(shape):
    nd = len(shape)
    return pl.BlockSpec(shape, lambda *_: (0,) * nd, pipeline_mode=pl.Buffered(1))


def _layer_spec(shape, layer):
    nd = len(shape)
    return pl.BlockSpec((None,) + tuple(shape), lambda *_: (layer,) + (0,) * nd,
                        pipeline_mode=pl.Buffered(1))


def _params(*sem):
    return pltpu.CompilerParams(dimension_semantics=sem, vmem_limit_bytes=VMEM_LIMIT_BYTES)


def _sel(r, cols=slice(None)):
    return r + (cols,)


def _chained(block_shape, head, tail):
    *lead, n_rows, _ = block_shape
    rows = [tuple(b) + (slice(r, r + CHAIN_ROWS),)
            for b in ([(i,) for i in range(lead[0])] if lead else [()])
            for r in range(0, n_rows, CHAIN_ROWS)]
    nxt = head(rows[0])
    for i, r in enumerate(rows):
        cur = nxt
        if i + 1 < len(rows):
            nxt = head(rows[i + 1])
        tail(r, *cur)


def _mod_kernel(c_ref, w_ref, b_ref, o_ref):
    a = _silu(c_ref[...]).astype(bf16)
    o_ref[...] = _dot(a, w_ref[...].astype(bf16)) + b_ref[...]


def _modulation(c16, w_mod, b_mod):
    depth, d, n = w_mod.shape
    tn = 1536
    return pl.pallas_call(
        _mod_kernel,
        out_shape=jax.ShapeDtypeStruct((depth, 16, n), f32),
        grid=(depth, n // tn),
        in_specs=[
            pl.BlockSpec((16, d), lambda l, j: (0, 0)),
            pl.BlockSpec((None, d, tn), lambda l, j: (l, 0, j)),
            pl.BlockSpec((None, 1, tn), lambda l, j: (l, 0, j)),
        ],
        out_specs=pl.BlockSpec((None, 16, tn), lambda l, j: (l, 0, j)),
        compiler_params=_params("arbitrary", "arbitrary"),
        name="modulation",
    )(c16, w_mod, b_mod.reshape(depth, 1, n))


def _rmsnorm_mod(x, g, scale, shift):
    y = x * lax.rsqrt(jnp.mean(x * x, axis=-1, keepdims=True) + EPS) * g
    return y * (1.0 + scale) + shift


def _rope(x, cos, sin_signed):
    lane = lax.broadcasted_iota(jnp.int32, x.shape, 1)
    first = (lane % 32) < 16
    partner = jnp.where(first, pltpu.roll(x, LANES - 16, 1), pltpu.roll(x, 16, 1))
    return x * cos + partner * sin_signed


def _inproj_kernel(x_ref, mod_ref, g_ref, w_ref, cos_ref, sin_ref, qn_ref, kn_ref, ones_ref,
                   *refs):
    pool_ref, ssm_ref, q_ref, k2_ref, v2_ref = refs[-5:]

    def norm(r):
        return (_rmsnorm_mod(x_ref[_sel(r)], g_ref[...], mod_ref[1:2, :], mod_ref[0:1, :]).astype(bf16),)

    def project(r, h):
        proj = _dot(h, w_ref[...])
        pool_ref[_sel(r)] = proj[:, 0:POOL_WIDTH]
        ssm_ref[_sel(r)] = proj[:, POOL_WIDTH:POOL_WIDTH + SSM_WIDTH]
        o = POOL_WIDTH + SSM_WIDTH
        q = proj[:, o:o + ATTN_WIDTH]
        k = proj[:, o + ATTN_WIDTH:o + ATTN_WIDTH + KV_WIDTH]
        v = proj[:, o + ATTN_WIDTH + KV_WIDTH:o + ATTN_WIDTH + 2 * KV_WIDTH]
        cos = cos_ref[r[-1], :]
        sin = sin_ref[r[-1], :]
        q_ms = _dot((q * q).astype(bf16), ones_ref[...])
        qn = q * lax.rsqrt(q_ms + EPS) * qn_ref[...]
        for cblk in range(ATTN_WIDTH // LANES):
            sl = slice(cblk * LANES, (cblk + 1) * LANES)
            q_ref[_sel(r, sl)] = _rope(qn[:, sl], cos, sin).astype(bf16)
        k_ms = _dot((k * k).astype(bf16), ones_ref[0:KV_WIDTH, 0:KV_WIDTH])
        kr = _rope(k * lax.rsqrt(k_ms + EPS) * kn_ref[...], cos, sin)
        lane = lax.broadcasted_iota(jnp.int32, kr.shape, 1)
        left = lane < HEAD_DIM
        ksw = pltpu.roll(kr, HEAD_DIM, 1)
        k2_ref[_sel(r, slice(0, LANES))] = jnp.where(left, kr, ksw).astype(bf16)
        k2_ref[_sel(r, slice(LANES, 2 * LANES))] = jnp.where(left, ksw, kr).astype(bf16)
        vsw = pltpu.roll(v, HEAD_DIM, 1)
        ones_col = jnp.where(lane == HEAD_DIM, 1.0, 0.0)
        v2_ref[_sel(r, slice(0, LANES))] = jnp.where(left, v, ones_col).astype(bf16)
        v2_ref[_sel(r, slice(LANES, 2 * LANES))] = jnp.where(left, vsw, ones_col).astype(bf16)

    _chained(x_ref.shape, norm, project)


def _inproj(layer, src, mod, mod_row, g, w_in, cos_t, sin_t, qn, kn, ones, n_total, row0, block,
            batches=None, prev=None):
    B, n_src, D = src.shape
    j0 = row0 // block
    win = w_in.shape[2]
    widths = (POOL_WIDTH, SSM_WIDTH, ATTN_WIDTH, 2 * KV_WIDTH, 2 * KV_WIDTH)
    dtypes = (f32, f32, bf16, bf16, bf16)
    in_specs = [
        pl.BlockSpec((batches, block, D), lambda b, j: (b, j, 0)),
        pl.BlockSpec((None, None, N_MOD, D), lambda b, j: (layer, mod_row(b), 0, 0)),
        _layer_spec((1, D), layer),
        _layer_spec((D, win), layer),
        pl.BlockSpec((block, LANES), lambda b, j: (j + j0, 0)),
        pl.BlockSpec((block, LANES), lambda b, j: (j + j0, 0)),
        _layer_spec((1, ATTN_WIDTH), layer),
        _layer_spec((1, KV_WIDTH), layer),
        _const_spec((ATTN_WIDTH, ATTN_WIDTH)),
    ]
    args = [src, mod, g, w_in, cos_t, sin_t, qn, kn, ones]
    aliases = {}
    if prev is not None:
        aliases = {len(args) + i: i for i in range(len(prev))}
        in_specs += [pl.BlockSpec(memory_space=pl.ANY)] * len(prev)
        args += list(prev)
    return pl.pallas_call(
        _inproj_kernel,
        out_shape=tuple(jax.ShapeDtypeStruct((B, n_total, w), dt) for w, dt in zip(widths, dtypes)),
        grid=(B // (batches or 1), n_src // block),
        in_specs=in_specs,
        out_specs=tuple(pl.BlockSpec((batches, block, w), lambda b, j: (b, j + j0, 0))
                        for w in widths),
        input_output_aliases=aliases,
        compiler_params=_params("arbitrary", "arbitrary"),
        name="inproj",
    )(*args)


def _shift_rows(x, s, row):
    n = x.shape[0]
    if s > 0:
        return jnp.where(row >= s, pltpu.roll(x, s, 0), 0.0)
    return jnp.where(row < n + s, pltpu.roll(x, n + s, 0), 0.0)


def _window_sums(u, half_a, half_b, lane_split):
    row = lax.broadcasted_iota(jnp.int32, u.shape, 0)
    lane = lax.broadcasted_iota(jnp.int32, u.shape, 1)
    bwd = u
    fwd = u
    k = 1
    out_a = None
    while True:
        if k == half_a:
            out_a = _shift_rows(bwd, 1, row) + fwd
        if k == half_b:
            out_b = _shift_rows(bwd, 1, row) + fwd
            break
        bwd = bwd + _shift_rows(bwd, k, row)
        fwd = fwd + _shift_rows(fwd, -k, row)
        k *= 2
    n = u.shape[0]
    half = jnp.where(lane < lane_split, half_a, half_b)
    cnt = jnp.minimum(row + half, n) - jnp.maximum(row - half, 0)
    return jnp.where(lane < lane_split, out_a, out_b) / cnt.astype(f32)


def _pool_segment(u_ref, w_ref, scale_ref, o_ref, r0, n):
    parts = []
    for blk in range(POOL_WIDTH // LANES):
        u = u_ref[r0:r0 + n, blk * LANES:(blk + 1) * LANES]
        wa, wb = POOL_WINDOWS[2 * blk], POOL_WINDOWS[2 * blk + 1]
        mean = _window_sums(u, wa // 2, wb // 2, POOL_GROUP)
        parts.append((mean - u).astype(bf16))
    d = jnp.concatenate(parts, axis=1)
    o_ref[r0:r0 + n, :] = (_dot(d, w_ref[...]) * scale_ref[...]).astype(bf16)


def _pool_kernel(u_ref, w_ref, scale_ref, o_ref, *, n_lat):
    nt = u_ref.shape[0]
    _pool_segment(u_ref, w_ref, scale_ref, o_ref, 0, n_lat)
    _pool_segment(u_ref, w_ref, scale_ref, o_ref, n_lat, nt - n_lat)


def _pool(layer, pool_u, w_bd, scale, n_lat):
    B, NT, W = pool_u.shape
    return pl.pallas_call(
        functools.partial(_pool_kernel, n_lat=n_lat),
        out_shape=jax.ShapeDtypeStruct((B, NT, W), bf16),
        grid=(B,),
        in_specs=[
            pl.BlockSpec((None, NT, W), lambda b: (b, 0, 0)),
            _layer_spec((W, W), layer),
            _layer_spec((1, W), layer),
        ],
        out_specs=pl.BlockSpec((None, NT, W), lambda b: (b, 0, 0)),
        compiler_params=_params("arbitrary"),
        name="pool",
    )(pool_u, w_bd, scale)


def _ssm_kernel(u_ref, bmat_ref, cmat_ref, are_ref, aim_ref, y_ref, upad_ref, state_ref,
                *bufs, n_chunks):
    g = pl.program_id(0)
    nb = u_ref.shape[0]
    ts, pitch = SCAN_STEPS, SCAN_PITCH
    n_seq = 2 * n_chunks
    nk = SSM_LANES // LANES
    bu_even, bu_odd, s_even, s_odd = bufs[0:2], bufs[2:4], bufs[4:6], bufs[6:8]

    @pl.when(g == 0)
    def _():
        for r in (upad_ref, state_ref) + tuple(bufs):
            r[...] = jnp.zeros_like(r)

    def direction(i):
        return jnp.clip(i, 0, n_seq - 1) // n_chunks

    def half_rows(i, h):
        off = jnp.where(direction(i) == 0, h, 1 - h) * ts
        return pl.ds(pl.multiple_of(off, ts), ts)

    def stage_b(i, h, bu_ref):
        rows = half_rows(i, h)
        for b in range(nb):
            upad_ref[b * pitch:b * pitch + ts, :] = u_ref[b, rows, :]
        res = _dot(upad_ref[...].astype(bf16), bmat_ref[direction(i)])
        for k in range(2 * nk):
            bu_ref[k] = res[:, k * LANES:(k + 1) * LANES]

    def stage_scan(i, bu_ref, s_ref):
        ic = jnp.clip(i, 0, n_seq - 1)
        d = ic // n_chunks
        keep = jnp.where((ic % n_chunks) == 0, 0.0, 1.0)
        a_re = are_ref[d]
        a_im = aim_ref[d]
        st_re = [state_ref[:, k * LANES:(k + 1) * LANES] * keep for k in range(nk)]
        st_im = [state_ref[:, (nk + k) * LANES:(nk + k + 1) * LANES] * keep for k in range(nk)]
        for t in range(ts):
            tt = jnp.where(d == 0, t, ts - 1 - t)
            rows = pl.ds(tt, nb, stride=pitch)
            for k in range(nk):
                ar = a_re[:, k * LANES:(k + 1) * LANES]
                ai = a_im[:, k * LANES:(k + 1) * LANES]
                n_re = ar * st_re[k] - ai * st_im[k] + bu_ref[k, rows, :]
                n_im = ar * st_im[k] + ai * st_re[k] + bu_ref[nk + k, rows, :]
                s_ref[k, rows, :] = n_re
                s_ref[nk + k, rows, :] = n_im
                st_re[k], st_im[k] = n_re, n_im
        for k in range(nk):
            state_ref[:, k * LANES:(k + 1) * LANES] = st_re[k]
            state_ref[:, (nk + k) * LANES:(nk + k + 1) * LANES] = st_im[k]

    def stage_c(i, h, s_ref):
        s = jnp.concatenate([s_ref[k] for k in range(2 * nk)], axis=1).astype(bf16)
        yv = _dot(s, cmat_ref[direction(i)])
        rows = half_rows(i, h)
        for b in range(nb):
            y_ref[b, rows, :] = yv[b * pitch:b * pitch + ts, :]

    def step(bu_write, bu_read, s_write, s_read):
        for h in range(2):
            stage_scan(2 * g - 2 + h, bu_read[h], s_write[h])
            stage_b(2 * g + h, h, bu_write[h])
            stage_c(2 * g - 4 + h, h, s_read[h])

    @pl.when(g % 2 == 0)
    def _():
        step(bu_even, bu_odd, s_even, s_odd)

    @pl.when(g % 2 == 1)
    def _():
        step(bu_odd, bu_even, s_odd, s_even)


def _ssm(layer, u, bmat, cmat, a_re, a_im, n_lat):
    B, NT, W = u.shape
    ts = SCAN_STEPS
    n_chunks = NT // ts
    n_blocks = n_chunks // 2
    nx = n_lat // (2 * ts)
    nc = n_blocks - nx

    def block(step):
        step = jnp.clip(step, 0, 2 * n_blocks - 1)
        d = step // n_blocks
        j = step % n_blocks
        fwd = jnp.where(j < nc, nx + j, j - nc)
        back = jnp.where(j < nc, n_blocks - 1 - j, nx - 1 - (j - nc))
        return d, jnp.where(d == 0, fwd, back)

    rows = B * SCAN_PITCH
    slab = pltpu.VMEM((2 * SSM_LANES // LANES, rows, LANES), f32)
    return pl.pallas_call(
        functools.partial(_ssm_kernel, n_chunks=n_chunks),
        out_shape=jax.ShapeDtypeStruct((2, B, NT, W), f32),
        grid=(2 * n_blocks + 2,),
        in_specs=[
            pl.BlockSpec((B, 2 * ts, W), lambda g: (0, block(g)[1], 0)),
            _layer_spec((2, W, 2 * SSM_LANES), layer),
            _layer_spec((2, 2 * SSM_LANES, W), layer),
            _layer_spec((2, B, SSM_LANES), layer),
            _layer_spec((2, B, SSM_LANES), layer),
        ],
        out_specs=pl.BlockSpec((None, B, 2 * ts, W),
                               lambda g: (block(g - 2)[0], 0, block(g - 2)[1], 0)),
        scratch_shapes=[pltpu.VMEM((rows, W), f32), pltpu.VMEM((B, 2 * SSM_LANES), f32)]
        + [slab] * 8,
        compiler_params=_params("arbitrary"),
        name="ssm_scan",
    )(u, bmat, cmat, a_re, a_im)


def _attend_group(q_pairs, keys, vals, masks, sinks):
    nq = q_pairs[0].shape[0]
    lane = lax.broadcasted_iota(jnp.int32, (nq, LANES), 1)
    low = lane < HEAD_DIM
    zero = jnp.zeros((nq, LANES), bf16)
    lhs = jnp.concatenate([jnp.where(keep, qp, zero) for qp in q_pairs for keep in (low, ~low)],
                          axis=0)
    scores = [_dot_nt(lhs, kk) for kk in keys]
    scores = [s if mk is None else s + mk for s, mk in zip(scores, masks)]
    row = lax.broadcasted_iota(jnp.int32, (4 * nq, 1), 0)
    sink = jnp.where(row < 2 * nq, jnp.where(row < nq, sinks[0], sinks[1]),
                     jnp.where(row < 3 * nq, sinks[2], sinks[3])) * LOG2_E
    cols = [s[:, c:c + LANES] for s in scores for c in range(0, s.shape[1], LANES)]
    m = jnp.maximum(sink, jnp.max(functools.reduce(jnp.maximum, cols), axis=-1, keepdims=True))
    out = jnp.zeros((4 * nq, LANES), f32)
    for s, vv in zip(scores, vals):
        out = out + _dot(jnp.exp2(s - m).astype(bf16), vv)
    out = out / (out[:, HEAD_DIM:HEAD_DIM + 1] + jnp.exp2(sink - m))
    blocks = [out[h * nq:(h + 1) * nq] for h in range(4)]
    return [jnp.where(low, blocks[2 * p], pltpu.roll(blocks[2 * p + 1], HEAD_DIM, 1))
            for p in range(2)]


def _attn_kernel(sink_ref, q_ref, k2_ref, v2_ref, o_ref, *, n_lat):
    i = pl.program_id(1)
    nq = Q_TILE
    n_sub = q_ref.shape[0] // nq
    n_ctx = k2_ref.shape[0] - n_lat
    n_lat_steps = n_lat // (n_sub * nq)
    span = 3 * WINDOW

    def run(rq, keys_rows, masks):
        for hk in range(N_KV_HEADS):
            keys = [k2_ref[r, hk * LANES:(hk + 1) * LANES] for r in keys_rows]
            vals = [v2_ref[r, hk * LANES:(hk + 1) * LANES] for r in keys_rows]
            c0 = hk * GROUP * HEAD_DIM
            outs = _attend_group([q_ref[rq, c0:c0 + LANES], q_ref[rq, c0 + LANES:c0 + 2 * LANES]],
                                 keys, vals, masks, [sink_ref[hk * GROUP + h] for h in range(GROUP)])
            for p, o in enumerate(outs):
                o_ref[rq, c0 + p * LANES:c0 + (p + 1) * LANES] = o.astype(bf16)

    ctx_rows = pl.ds(n_lat, n_ctx)

    @pl.when(i >= n_lat_steps)
    def _():
        for sub in range(n_sub):
            run(slice(sub * nq, (sub + 1) * nq), [ctx_rows], [None])

    @pl.when(i < n_lat_steps)
    def _():
        for sub in range(n_sub):
            q0 = (i * n_sub + sub) * nq
            start = jnp.clip(q0 - WINDOW, 0, n_lat - span)
            q_pos = q0 + lax.broadcasted_iota(jnp.int32, (GROUP * nq, span), 0) % nq
            k_pos = start + lax.broadcasted_iota(jnp.int32, (GROUP * nq, span), 1)
            bias = jnp.where(jnp.abs(k_pos - q_pos) <= WINDOW, 0.0, -jnp.inf)
            win_rows = pl.ds(pl.multiple_of(start, WINDOW), span)
            run(slice(sub * nq, (sub + 1) * nq), [win_rows, ctx_rows], [bias, None])


def _attention(layer, sink, q, k2, v2, n_lat):
    B, NT, _ = q.shape
    return pl.pallas_call(
        functools.partial(_attn_kernel, n_lat=n_lat),
        out_shape=jax.ShapeDtypeStruct((B, NT, ATTN_WIDTH), bf16),
        grid=(B, NT // Q_BLOCK),
        in_specs=[
            pl.BlockSpec(memory_space=pltpu.SMEM),
            pl.BlockSpec((None, Q_BLOCK, ATTN_WIDTH), lambda b, i: (b, i, 0)),
            pl.BlockSpec((None, NT, 2 * KV_WIDTH), lambda b, i: (b, 0, 0)),
            pl.BlockSpec((None, NT, 2 * KV_WIDTH), lambda b, i: (b, 0, 0)),
        ],
        out_specs=pl.BlockSpec((None, Q_BLOCK, ATTN_WIDTH), lambda b, i: (b, i, 0)),
        compiler_params=_params("arbitrary", "arbitrary"),
        name="attention",
    )(sink[layer], q, k2, v2)


def _outffn_kernel(x_ref, mod_ref, po_ref, u_ref, y0_ref, y1_ref, ao_ref, d_ref, gw_ref, gb_ref,
                   wo_ref, g_ref, wg_ref, wu_ref, wd_ref, o_ref):
    o1 = POOL_WIDTH
    o2 = POOL_WIDTH + SSM_WIDTH
    d_ff = wg_ref.shape[1]

    def mixer(r):
        y = d_ref[...] * u_ref[_sel(r)] + y0_ref[_sel(r)] + y1_ref[_sel(r)]
        ge = _gelu_tanh(y)
        so = ge * _sigmoid(_dot(ge.astype(bf16), gw_ref[...]) + gb_ref[...])
        mix = (_dot(po_ref[_sel(r)], wo_ref[0:o1, :]) + _dot(so.astype(bf16), wo_ref[o1:o2, :])
               + _dot(ao_ref[_sel(r)], wo_ref[o2:, :]))
        x1 = x_ref[_sel(r)] + mod_ref[2:3, :] * mix
        return x1, _rmsnorm_mod(x1, g_ref[...], mod_ref[4:5, :], mod_ref[3:4, :]).astype(bf16)

    def ffn(r, x1, h):
        acc = jnp.zeros_like(x1)
        for c0 in range(0, d_ff, FF_CHUNK):
            c1 = min(c0 + FF_CHUNK, d_ff)
            gate = _dot(h, wg_ref[:, c0:c1])
            up = _dot(h, wu_ref[:, c0:c1])
            acc = acc + _dot((_silu(gate) * up).astype(bf16), wd_ref[c0:c1, :])
        o_ref[_sel(r)] = x1 + mod_ref[5:6, :] * acc

    _chained(x_ref.shape, mixer, ffn)


def _outffn(layer, src, mod, mod_row, po, u, y, ao, d_skip, glu_w, glu_b, w_out, g, w_gate, w_up,
            w_down, row0, block, batches=None):
    B, n_src, D = src.shape
    j0 = row0 // block
    d_ff = w_gate.shape[2]
    tok = lambda w: pl.BlockSpec((batches, block, w), lambda b, j: (b, j + j0, 0))
    ydir = lambda d: pl.BlockSpec((None, batches, block, SSM_WIDTH),
                                  lambda b, j: (d, b, j + j0, 0))
    return pl.pallas_call(
        _outffn_kernel,
        out_shape=jax.ShapeDtypeStruct((B, n_src, D), f32),
        grid=(B // (batches or 1), n_src // block),
        in_specs=[
            pl.BlockSpec((batches, block, D), lambda b, j: (b, j, 0)),
            pl.BlockSpec((None, None, N_MOD, D), lambda b, j: (layer, mod_row(b), 0, 0)),
            tok(POOL_WIDTH),
            tok(SSM_WIDTH),
            ydir(0),
            ydir(1),
            tok(ATTN_WIDTH),
            _layer_spec((1, SSM_WIDTH), layer),
            _layer_spec((SSM_WIDTH, SSM_WIDTH), layer),
            _layer_spec((1, SSM_WIDTH), layer),
            _layer_spec((D, D), layer),
            _layer_spec((1, D), layer),
            _layer_spec((D, d_ff), layer),
            _layer_spec((D, d_ff), layer),
            _layer_spec((d_ff, D), layer),
        ],
        out_specs=pl.BlockSpec((batches, block, D), lambda b, j: (b, j, 0)),
        compiler_params=_params("arbitrary", "arbitrary"),
        name="outproj_ffn",
    )(src, mod, po, u, y, y, ao, d_skip, glu_w, glu_b, w_out, g, w_gate, w_up, w_down)


def _rope_tables(n_lat, n_ctx):
    pos = jnp.arange(n_lat)
    row = (pos // GRID_W).astype(f32)
    col = (pos % GRID_W).astype(f32)
    inv = jnp.power(ROPE_BASE, -jnp.arange(ROPE_FREQS, dtype=f32) / ROPE_FREQS)
    lane = jnp.arange(LANES)
    dim = lane % HEAD_DIM
    freq = inv[dim % ROPE_FREQS]
    ang = jnp.where((dim // (2 * ROPE_FREQS)) == 0, row[:, None], col[:, None]) * freq[None, :]
    sign = jnp.where((dim % (2 * ROPE_FREQS)) < ROPE_FREQS, -1.0, 1.0).astype(f32)
    cos = jnp.concatenate([jnp.cos(ang), jnp.ones((n_ctx, LANES), f32)], axis=0)
    sin = jnp.concatenate([jnp.sin(ang) * sign, jnp.zeros((n_ctx, LANES), f32)], axis=0)
    return cos, sin


def _block_diag(blocks):
    *lead, g, r, c = blocks.shape
    on_diag = (jnp.arange(g)[:, None, None, None] == jnp.arange(g)[None, None, :, None])
    spread = jnp.where(on_diag, blocks[..., :, :, None, :], jnp.zeros((), blocks.dtype))
    return spread.reshape(*lead, g * r, g * c)


def _ssm_matrices(a_re, a_im, log_dt, b_re, b_im, c_re, c_im, n_batch):
    lr = jnp.minimum(a_re.astype(f32), -1e-4)
    li = a_im.astype(f32)
    dt = jnp.exp(log_dt.astype(f32))[..., None]
    mag = jnp.exp(lr * dt)
    ar = mag * jnp.cos(li * dt)
    ai = mag * jnp.sin(li * dt)
    den = lr * lr + li * li
    qr = ((ar - 1.0) * lr + ai * li) / den
    qi = (ai * lr - (ar - 1.0) * li) / den
    bbr = qr[..., None] * b_re - qi[..., None] * b_im
    bbi = qr[..., None] * b_im + qi[..., None] * b_re
    bd = lambda a: _block_diag(jnp.swapaxes(a, -1, -2).astype(bf16))
    bmat = jnp.concatenate([bd(bbr), bd(bbi)], axis=-1)
    cmat = jnp.concatenate([bd(c_re), bd(-c_im)], axis=-2)
    shape = a_re.shape[:2] + (n_batch, SSM_LANES)
    a_re_b = jnp.broadcast_to(ar.reshape(a_re.shape[:2] + (1, SSM_LANES)), shape)
    a_im_b = jnp.broadcast_to(ai.reshape(a_re.shape[:2] + (1, SSM_LANES)), shape)
    return bmat.astype(bf16), cmat.astype(bf16), a_re_b, a_im_b


def kernel(x, c, ctx, c_ctx, w_mod, b_mod, norm_mix, norm_ffn, w_in, w_out, pool_w, pool_scale, ssm_a_re, ssm_a_im, ssm_log_dt, ssm_b_re, ssm_b_im, ssm_c_re, ssm_c_im, ssm_d, ssm_glu_w, ssm_glu_b, q_norm, k_norm, attn_sink, ffn_w_gate, ffn_w_up, ffn_w_down):
    B, L, D = x.shape
    Lc = ctx.shape[1]
    NT = L + Lc
    depth = w_mod.shape[0]
    assert Lc == CHAIN_ROWS and L % LATENT_BLOCK == 0 and L % GRID_W == 0 and L >= 3 * WINDOW
    assert B == SUBLANES and w_in.shape[2] == POOL_WIDTH + SSM_WIDTH + ATTN_WIDTH + 2 * KV_WIDTH
    assert w_mod.shape[2] == N_MOD * D and Lc % (2 * SCAN_STEPS) == 0 and L % (2 * SCAN_STEPS) == 0

    c16 = jnp.concatenate([c, c_ctx[None, :], jnp.zeros((16 - B - 1, D), f32)], axis=0)
    mod = _modulation(c16, w_mod, b_mod).reshape(depth, 16, N_MOD, D)
    lat_row = lambda b: b
    ctx_row = lambda b: B

    cos_t, sin_t = _rope_tables(L, Lc)
    head = jnp.arange(ATTN_WIDTH) // HEAD_DIM
    ones = jnp.where(head[:, None] == head[None, :], 1.0 / HEAD_DIM, 0.0).astype(bf16)
    qn = (jnp.tile(q_norm, (1, N_HEADS)) * (HEAD_DIM ** -0.5 * LOG2_E)).reshape(depth, 1, ATTN_WIDTH)
    kn = jnp.tile(k_norm, (1, N_KV_HEADS)).reshape(depth, 1, KV_WIDTH)
    bmat, cmat, a_re, a_im = _ssm_matrices(ssm_a_re, ssm_a_im, ssm_log_dt, ssm_b_re, ssm_b_im,
                                           ssm_c_re, ssm_c_im, B)
    pool_bd = _block_diag(pool_w).astype(bf16)
    w_in_b, w_out_b, glu_w_b = w_in.astype(bf16), w_out.astype(bf16), ssm_glu_w.astype(bf16)
    w_gate_b, w_up_b, w_down_b = ffn_w_gate.astype(bf16), ffn_w_up.astype(bf16), ffn_w_down.astype(bf16)
    row = lambda a: a.reshape(depth, 1, a.shape[-1])

    for l in range(depth):
        proj_args = (row(norm_mix), w_in_b, cos_t, sin_t, qn, kn, ones, NT)
        outs = _inproj(l, x, mod, lat_row, *proj_args, 0, LATENT_BLOCK)
        pool_u, u, q, k2, v2 = _inproj(l, ctx, mod, ctx_row, *proj_args, L, CHAIN_ROWS,
                                       batches=CTX_BATCHES, prev=outs)
        po = _pool(l, pool_u, pool_bd, row(pool_scale), L)
        y = _ssm(l, u, bmat, cmat, a_re, a_im, L)
        ao = _attention(l, attn_sink, q, k2, v2, L)
        ffn_args = (po, u, y, ao, row(ssm_d), glu_w_b, row(ssm_glu_b), w_out_b, row(norm_ffn),
                    w_gate_b, w_up_b, w_down_b)
        x_new = _outffn(l, x, mod, lat_row, *ffn_args, 0, LATENT_BLOCK)
        if l < depth - 1:
            ctx = _outffn(l, ctx, mod, ctx_row, *ffn_args, L, CHAIN_ROWS, batches=CTX_BATCHES)
        x = x_new
    return x
```

```python
import functools
import math

import jax
import jax.numpy as jnp
from jax import lax
from jax.experimental import pallas as pl
from jax.experimental.pallas import tpu as pltpu

f32 = jnp.float32
bf16 = jnp.bfloat16

GRID_W = 64
POOL_WINDOWS = (2, 4, 8, 16)
POOL_GROUP = 64
POOL_WIDTH = POOL_GROUP * len(POOL_WINDOWS)
SSM_GROUP_CH = 16
SSM_GROUPS = 16
SSM_WIDTH = SSM_GROUP_CH * SSM_GROUPS
SSM_STATE = 64
SSM_LANES = SSM_GROUPS * SSM_STATE
HEAD_DIM = 64
N_HEADS = 8
N_KV_HEADS = 2
GROUP = N_HEADS // N_KV_HEADS
ATTN_WIDTH = N_HEADS * HEAD_DIM
KV_WIDTH = N_KV_HEADS * HEAD_DIM
WINDOW = 128
ROPE_BASE = 10000.0
ROPE_FREQS = HEAD_DIM // 4
N_MOD = 6
EPS = 1e-6
LOG2_E = math.log2(math.e)

LANES = 128
SUBLANES = 8
MXU_TILE = 256
CHAIN_ROWS = 256
LATENT_BLOCK = 4 * CHAIN_ROWS
Q_TILE = 128
Q_BLOCK = 2 * Q_TILE
SCAN_STEPS = 64
SCAN_PITCH = SCAN_STEPS + 4
FF_CHUNK = 6 * MXU_TILE
VMEM_LIMIT_BYTES = 56 * 1024 * 1024


def _sigmoid(x):
    return 1.0 / (1.0 + jnp.exp(-x))


def _silu(x):
    return x * _sigmoid(x)


def _gelu_tanh(x):
    return 0.5 * x * (1.0 + jnp.tanh(math.sqrt(2.0 / math.pi) * (x + 0.044715 * (x * x * x))))


def _dot(a, b):
    return jnp.dot(a, b, preferred_element_type=f32)


def _dot_nt(a, b):
    return lax.dot_general(a, b, (((1,), (1,)), ((), ())), preferred_element_type=f32)


def _const_spec(shape):
    nd = len(shape)
    return pl.BlockSpec(shape, lambda *_: (0,) * nd, pipeline_mode=pl.Buffered(1))


def _layer_spec(shape, layer):
    nd = len(shape)
    return pl.BlockSpec((None,) + tuple(shape), lambda *_: (layer,) + (0,) * nd,
                        pipeline_mode=pl.Buffered(1))


def _params(*sem):
    return pltpu.CompilerParams(dimension_semantics=sem, vmem_limit_bytes=VMEM_LIMIT_BYTES)


def _chained(n_rows, head, tail):
    rows = [slice(r, r + CHAIN_ROWS) for r in range(0, n_rows, CHAIN_ROWS)]
    nxt = head(rows[0])
    for i, r in enumerate(rows):
        cur = nxt
        if i + 1 < len(rows):
            nxt = head(rows[i + 1])
        tail(r, *cur)


def _mod_kernel(c_ref, w_ref, b_ref, o_ref):
    a = _silu(c_ref[...]).astype(bf16)
    o_ref[...] = _dot(a, w_ref[...].astype(bf16)) + b_ref[...]


def _modulation(c16, w_mod, b_mod):
    depth, d, n = w_mod.shape
    tn = 1536
    return pl.pallas_call(
        _mod_kernel,
        out_shape=jax.ShapeDtypeStruct((depth, 16, n), f32),
        grid=(depth, n // tn),
        in_specs=[
            pl.BlockSpec((16, d), lambda l, j: (0, 0)),
            pl.BlockSpec((None, d, tn), lambda l, j: (l, 0, j)),
            pl.BlockSpec((None, 1, tn), lambda l, j: (l, 0, j)),
        ],
        out_specs=pl.BlockSpec((None, 16, tn), lambda l, j: (l, 0, j)),
        compiler_params=_params("arbitrary", "arbitrary"),
        name="modulation",
    )(c16, w_mod, b_mod.reshape(depth, 1, n))


def _rmsnorm_mod(x, g, scale, shift):
    y = x * lax.rsqrt(jnp.mean(x * x, axis=-1, keepdims=True) + EPS) * g
    return y * (1.0 + scale) + shift


def _rope(x, cos, sin_signed):
    lane = lax.broadcasted_iota(jnp.int32, x.shape, 1)
    first = (lane % 32) < 16
    partner = jnp.where(first, pltpu.roll(x, LANES - 16, 1), pltpu.roll(x, 16, 1))
    return x * cos + partner * sin_signed


def _inproj_kernel(x_ref, c_ref, mod_ref, g_ref, w_ref, cos_ref, sin_ref, qn_ref, kn_ref, ones_ref,
                   pool_ref, ssm_ref, q_ref, k2_ref, v2_ref):
    is_ctx = pl.program_id(1) == pl.num_programs(1) - 1

    @pl.when(jnp.logical_not(is_ctx))
    def _():
        _inproj_rows(x_ref, mod_ref, g_ref, w_ref, cos_ref, sin_ref, qn_ref, kn_ref, ones_ref,
                     pool_ref, ssm_ref, q_ref, k2_ref, v2_ref)

    @pl.when(is_ctx)
    def _():
        _inproj_rows(c_ref, mod_ref, g_ref, w_ref, cos_ref, sin_ref, qn_ref, kn_ref, ones_ref,
                     pool_ref, ssm_ref, q_ref, k2_ref, v2_ref)


def _inproj_rows(x_ref, mod_ref, g_ref, w_ref, cos_ref, sin_ref, qn_ref, kn_ref, ones_ref,
                 pool_ref, ssm_ref, q_ref, k2_ref, v2_ref):
    def norm(r):
        return (_rmsnorm_mod(x_ref[r, :], g_ref[...], mod_ref[1:2, :], mod_ref[0:1, :]).astype(bf16),)

    def project(r, h):
        proj = _dot(h, w_ref[...])
        pool_ref[r, :] = proj[:, 0:POOL_WIDTH]
        ssm_ref[r, :] = proj[:, POOL_WIDTH:POOL_WIDTH + SSM_WIDTH]
        o = POOL_WIDTH + SSM_WIDTH
        q = proj[:, o:o + ATTN_WIDTH]
        k = proj[:, o + ATTN_WIDTH:o + ATTN_WIDTH + KV_WIDTH]
        v = proj[:, o + ATTN_WIDTH + KV_WIDTH:o + ATTN_WIDTH + 2 * KV_WIDTH]
        cos = cos_ref[r, :]
        sin = sin_ref[r, :]
        q_ms = _dot((q * q).astype(bf16), ones_ref[...])
        qn = q * lax.rsqrt(q_ms + EPS) * qn_ref[...]
        for cblk in range(ATTN_WIDTH // LANES):
            sl = slice(cblk * LANES, (cblk + 1) * LANES)
            q_ref[r, sl] = _rope(qn[:, sl], cos, sin).astype(bf16)
        k_ms = _dot((k * k).astype(bf16), ones_ref[0:KV_WIDTH, 0:KV_WIDTH])
        kr = _rope(k * lax.rsqrt(k_ms + EPS) * kn_ref[...], cos, sin)
        lane = lax.broadcasted_iota(jnp.int32, kr.shape, 1)
        left = lane < HEAD_DIM
        ksw = pltpu.roll(kr, HEAD_DIM, 1)
        k2_ref[r, 0:LANES] = jnp.where(left, kr, ksw).astype(bf16)
        k2_ref[r, LANES:2 * LANES] = jnp.where(left, ksw, kr).astype(bf16)
        vsw = pltpu.roll(v, HEAD_DIM, 1)
        ones_col = jnp.where(lane == HEAD_DIM, 1.0, 0.0)
        v2_ref[r, 0:LANES] = jnp.where(left, v, ones_col).astype(bf16)
        v2_ref[r, LANES:2 * LANES] = jnp.where(left, vsw, ones_col).astype(bf16)

    _chained(x_ref.shape[0], norm, project)


def _mod_row(j, n_lat_steps, b, n_batch):
    return jnp.where(j == n_lat_steps, n_batch, b)


def _inproj(layer, x, ctx, mod, g, w_in, cos_t, sin_t, qn, kn, ones):
    B, L, D = x.shape
    Lc = ctx.shape[1]
    blk = LATENT_BLOCK
    nj = L // blk
    win = w_in.shape[2]
    widths = (POOL_WIDTH, SSM_WIDTH, ATTN_WIDTH, 2 * KV_WIDTH, 2 * KV_WIDTH)
    dtypes = (f32, f32, bf16, bf16, bf16)
    return pl.pallas_call(
        _inproj_kernel,
        out_shape=tuple(jax.ShapeDtypeStruct((B, L + Lc, w), dt) for w, dt in zip(widths, dtypes)),
        grid=(B, nj + 1),
        in_specs=[
            pl.BlockSpec((None, blk, D), lambda b, j: (b, jnp.minimum(j, nj - 1), 0)),
            pl.BlockSpec((None, Lc, D), lambda b, j: (b, 0, 0)),
            pl.BlockSpec((None, None, N_MOD, D), lambda b, j: (layer, _mod_row(j, nj, b, B), 0, 0)),
            _layer_spec((1, D), layer),
            _layer_spec((D, win), layer),
            pl.BlockSpec((blk, LANES), lambda b, j: (j, 0)),
            pl.BlockSpec((blk, LANES), lambda b, j: (j, 0)),
            _layer_spec((1, ATTN_WIDTH), layer),
            _layer_spec((1, KV_WIDTH), layer),
            _const_spec((ATTN_WIDTH, ATTN_WIDTH)),
        ],
        out_specs=tuple(pl.BlockSpec((None, blk, w), lambda b, j: (b, j, 0)) for w in widths),
        compiler_params=_params("arbitrary", "arbitrary"),
        name="inproj",
    )(x, ctx, mod, g, w_in, cos_t, sin_t, qn, kn, ones)


def _shift_rows(x, s, row):
    n = x.shape[0]
    if s > 0:
        return jnp.where(row >= s, pltpu.roll(x, s, 0), 0.0)
    return jnp.where(row < n + s, pltpu.roll(x, n + s, 0), 0.0)


def _window_sums(u, half_a, half_b, lane_split):
    row = lax.broadcasted_iota(jnp.int32, u.shape, 0)
    lane = lax.broadcasted_iota(jnp.int32, u.shape, 1)
    bwd = u
    fwd = u
    k = 1
    out_a = None
    while True:
        if k == half_a:
            out_a = _shift_rows(bwd, 1, row) + fwd
        if k == half_b:
            out_b = _shift_rows(bwd, 1, row) + fwd
            break
        bwd = bwd + _shift_rows(bwd, k, row)
        fwd = fwd + _shift_rows(fwd, -k, row)
        k *= 2
    n = u.shape[0]
    half = jnp.where(lane < lane_split, half_a, half_b)
    cnt = jnp.minimum(row + half, n) - jnp.maximum(row - half, 0)
    return jnp.where(lane < lane_split, out_a, out_b) / cnt.astype(f32)


def _pool_segment(u_ref, w_ref, scale_ref, o_ref, r0, n):
    parts = []
    for blk in range(POOL_WIDTH // LANES):
        u = u_ref[r0:r0 + n, blk * LANES:(blk + 1) * LANES]
        wa, wb = POOL_WINDOWS[2 * blk], POOL_WINDOWS[2 * blk + 1]
        mean = _window_sums(u, wa // 2, wb // 2, POOL_GROUP)
        parts.append((mean - u).astype(bf16))
    d = jnp.concatenate(parts, axis=1)
    o_ref[r0:r0 + n, :] = (_dot(d, w_ref[...]) * scale_ref[...]).astype(bf16)


def _pool_kernel(u_ref, w_ref, scale_ref, o_ref, *, n_lat):
    nt = u_ref.shape[0]
    _pool_segment(u_ref, w_ref, scale_ref, o_ref, 0, n_lat)
    _pool_segment(u_ref, w_ref, scale_ref, o_ref, n_lat, nt - n_lat)


def _pool(layer, pool_u, w_bd, scale, n_lat):
    B, NT, W = pool_u.shape
    return pl.pallas_call(
        functools.partial(_pool_kernel, n_lat=n_lat),
        out_shape=jax.ShapeDtypeStruct((B, NT, W), bf16),
        grid=(B,),
        in_specs=[
            pl.BlockSpec((None, NT, W), lambda b: (b, 0, 0)),
            _layer_spec((W, W), layer),
            _layer_spec((1, W), layer),
        ],
        out_specs=pl.BlockSpec((None, NT, W), lambda b: (b, 0, 0)),
        compiler_params=_params("arbitrary"),
        name="pool",
    )(pool_u, w_bd, scale)


def _ssm_kernel(u_ref, bmat_ref, cmat_ref, are_ref, aim_ref, y_ref, upad_ref, state_ref,
                *bufs, n_chunks):
    g = pl.program_id(0)
    nb = u_ref.shape[0]
    ts, pitch = SCAN_STEPS, SCAN_PITCH
    n_seq = 2 * n_chunks
    nk = SSM_LANES // LANES
    bu_even, bu_odd, s_even, s_odd = bufs[0:2], bufs[2:4], bufs[4:6], bufs[6:8]

    @pl.when(g == 0)
    def _():
        for r in (upad_ref, state_ref) + tuple(bufs):
            r[...] = jnp.zeros_like(r)

    def direction(i):
        return jnp.clip(i, 0, n_seq - 1) // n_chunks

    def half_rows(i, h):
        off = jnp.where(direction(i) == 0, h, 1 - h) * ts
        return pl.ds(pl.multiple_of(off, ts), ts)

    def stage_b(i, h, bu_ref):
        rows = half_rows(i, h)
        for b in range(nb):
            upad_ref[b * pitch:b * pitch + ts, :] = u_ref[b, rows, :]
        res = _dot(upad_ref[...].astype(bf16), bmat_ref[direction(i)])
        for k in range(2 * nk):
            bu_ref[k] = res[:, k * LANES:(k + 1) * LANES]

    def stage_scan(i, bu_ref, s_ref):
        ic = jnp.clip(i, 0, n_seq - 1)
        d = ic // n_chunks
        keep = jnp.where((ic % n_chunks) == 0, 0.0, 1.0)
        a_re = are_ref[d]
        a_im = aim_ref[d]
        st_re = [state_ref[:, k * LANES:(k + 1) * LANES] * keep for k in range(nk)]
        st_im = [state_ref[:, (nk + k) * LANES:(nk + k + 1) * LANES] * keep for k in range(nk)]
        for t in range(ts):
            tt = jnp.where(d == 0, t, ts - 1 - t)
            rows = pl.ds(tt, nb, stride=pitch)
            for k in range(nk):
                ar = a_re[:, k * LANES:(k + 1) * LANES]
                ai = a_im[:, k * LANES:(k + 1) * LANES]
                n_re = ar * st_re[k] - ai * st_im[k] + bu_ref[k, rows, :]
                n_im = ar * st_im[k] + ai * st_re[k] + bu_ref[nk + k, rows, :]
                s_ref[k, rows, :] = n_re
                s_ref[nk + k, rows, :] = n_im
                st_re[k], st_im[k] = n_re, n_im
        for k in range(nk):
            state_ref[:, k * LANES:(k + 1) * LANES] = st_re[k]
            state_ref[:, (nk + k) * LANES:(nk + k + 1) * LANES] = st_im[k]

    def stage_c(i, h, s_ref):
        s = jnp.concatenate([s_ref[k] for k in range(2 * nk)], axis=1).astype(bf16)
        yv = _dot(s, cmat_ref[direction(i)])
        rows = half_rows(i, h)
        for b in range(nb):
            y_ref[b, rows, :] = yv[b * pitch:b * pitch + ts, :]

    def step(bu_write, bu_read, s_write, s_read):
        for h in range(2):
            stage_scan(2 * g - 2 + h, bu_read[h], s_write[h])
            stage_b(2 * g + h, h, bu_write[h])
            stage_c(2 * g - 4 + h, h, s_read[h])

    @pl.when(g % 2 == 0)
    def _():
        step(bu_even, bu_odd, s_even, s_odd)

    @pl.when(g % 2 == 1)
    def _():
        step(bu_odd, bu_even, s_odd, s_even)


def _ssm(layer, u, bmat, cmat, a_re, a_im, n_lat):
    B, NT, W = u.shape
    ts = SCAN_STEPS
    n_chunks = NT // ts
    n_blocks = n_chunks // 2
    nx = n_lat // (2 * ts)
    nc = n_blocks - nx

    def block(step):
        step = jnp.clip(step, 0, 2 * n_blocks - 1)
        d = step // n_blocks
        j = step % n_blocks
        fwd = jnp.where(j < nc, nx + j, j - nc)
        back = jnp.where(j < nc, n_blocks - 1 - j, nx - 1 - (j - nc))
        return d, jnp.where(d == 0, fwd, back)

    rows = B * SCAN_PITCH
    slab = pltpu.VMEM((2 * SSM_LANES // LANES, rows, LANES), f32)
    return pl.pallas_call(
        functools.partial(_ssm_kernel, n_chunks=n_chunks),
        out_shape=jax.ShapeDtypeStruct((2, B, NT, W), f32),
        grid=(2 * n_blocks + 2,),
        in_specs=[
            pl.BlockSpec((B, 2 * ts, W), lambda g: (0, block(g)[1], 0)),
            _layer_spec((2, W, 2 * SSM_LANES), layer),
            _layer_spec((2, 2 * SSM_LANES, W), layer),
            _layer_spec((2, B, SSM_LANES), layer),
            _layer_spec((2, B, SSM_LANES), layer),
        ],
        out_specs=pl.BlockSpec((None, B, 2 * ts, W),
                               lambda g: (block(g - 2)[0], 0, block(g - 2)[1], 0)),
        scratch_shapes=[pltpu.VMEM((rows, W), f32), pltpu.VMEM((B, 2 * SSM_LANES), f32)]
        + [slab] * 8,
        compiler_params=_params("arbitrary"),
        name="ssm_scan",
    )(u, bmat, cmat, a_re, a_im)


def _attend_group(q_pairs, keys, vals, masks, sinks):
    nq = q_pairs[0].shape[0]
    lane = lax.broadcasted_iota(jnp.int32, (nq, LANES), 1)
    low = lane < HEAD_DIM
    zero = jnp.zeros((nq, LANES), bf16)
    lhs = jnp.concatenate([jnp.where(keep, qp, zero) for qp in q_pairs for keep in (low, ~low)],
                          axis=0)
    scores = [_dot_nt(lhs, kk) for kk in keys]
    scores = [s if mk is None else s + mk for s, mk in zip(scores, masks)]
    row = lax.broadcasted_iota(jnp.int32, (4 * nq, 1), 0)
    sink = jnp.where(row < 2 * nq, jnp.where(row < nq, sinks[0], sinks[1]),
                     jnp.where(row < 3 * nq, sinks[2], sinks[3])) * LOG2_E
    cols = [s[:, c:c + LANES] for s in scores for c in range(0, s.shape[1], LANES)]
    m = jnp.maximum(sink, jnp.max(functools.reduce(jnp.maximum, cols), axis=-1, keepdims=True))
    out = jnp.zeros((4 * nq, LANES), f32)
    for s, vv in zip(scores, vals):
        out = out + _dot(jnp.exp2(s - m).astype(bf16), vv)
    out = out / (out[:, HEAD_DIM:HEAD_DIM + 1] + jnp.exp2(sink - m))
    blocks = [out[h * nq:(h + 1) * nq] for h in range(4)]
    return [jnp.where(low, blocks[2 * p], pltpu.roll(blocks[2 * p + 1], HEAD_DIM, 1))
            for p in range(2)]


def _attn_kernel(sink_ref, q_ref, k2_ref, v2_ref, o_ref, *, n_lat):
    i = pl.program_id(1)
    nq = Q_TILE
    n_sub = q_ref.shape[0] // nq
    n_ctx = k2_ref.shape[0] - n_lat
    n_lat_steps = n_lat // (n_sub * nq)
    span = 3 * WINDOW

    def run(rq, keys_rows, masks):
        for hk in range(N_KV_HEADS):
            keys = [k2_ref[r, hk * LANES:(hk + 1) * LANES] for r in keys_rows]
            vals = [v2_ref[r, hk * LANES:(hk + 1) * LANES] for r in keys_rows]
            c0 = hk * GROUP * HEAD_DIM
            outs = _attend_group([q_ref[rq, c0:c0 + LANES], q_ref[rq, c0 + LANES:c0 + 2 * LANES]],
                                 keys, vals, masks, [sink_ref[hk * GROUP + h] for h in range(GROUP)])
            for p, o in enumerate(outs):
                o_ref[rq, c0 + p * LANES:c0 + (p + 1) * LANES] = o.astype(bf16)

    ctx_rows = pl.ds(n_lat, n_ctx)

    @pl.when(i >= n_lat_steps)
    def _():
        for sub in range(n_sub):
            run(slice(sub * nq, (sub + 1) * nq), [ctx_rows], [None])

    @pl.when(i < n_lat_steps)
    def _():
        for sub in range(n_sub):
            q0 = (i * n_sub + sub) * nq
            start = jnp.clip(q0 - WINDOW, 0, n_lat - span)
            q_pos = q0 + lax.broadcasted_iota(jnp.int32, (GROUP * nq, span), 0) % nq
            k_pos = start + lax.broadcasted_iota(jnp.int32, (GROUP * nq, span), 1)
            bias = jnp.where(jnp.abs(k_pos - q_pos) <= WINDOW, 0.0, -jnp.inf)
            win_rows = pl.ds(pl.multiple_of(start, WINDOW), span)
            run(slice(sub * nq, (sub + 1) * nq), [win_rows, ctx_rows], [bias, None])


def _attention(layer, sink, q, k2, v2, n_lat):
    B, NT, _ = q.shape
    return pl.pallas_call(
        functools.partial(_attn_kernel, n_lat=n_lat),
        out_shape=jax.ShapeDtypeStruct((B, NT, ATTN_WIDTH), bf16),
        grid=(B, NT // Q_BLOCK),
        in_specs=[
            pl.BlockSpec(memory_space=pltpu.SMEM),
            pl.BlockSpec((None, Q_BLOCK, ATTN_WIDTH), lambda b, i: (b, i, 0)),
            pl.BlockSpec((None, NT, 2 * KV_WIDTH), lambda b, i: (b, 0, 0)),
            pl.BlockSpec((None, NT, 2 * KV_WIDTH), lambda b, i: (b, 0, 0)),
        ],
        out_specs=pl.BlockSpec((None, Q_BLOCK, ATTN_WIDTH), lambda b, i: (b, i, 0)),
        compiler_params=_params("arbitrary", "arbitrary"),
        name="attention",
    )(sink[layer], q, k2, v2)


def _outffn_kernel(*refs, with_ctx):
    if not with_ctx:
        _outffn_rows(refs[0], *refs[1:-1], refs[-1])
        return
    x_ref, c_ref, shared, xo_ref, co_ref = refs[0], refs[1], refs[2:-2], refs[-2], refs[-1]
    is_ctx = pl.program_id(1) == pl.num_programs(1) - 1

    @pl.when(jnp.logical_not(is_ctx))
    def _():
        _outffn_rows(x_ref, *shared, xo_ref)

    @pl.when(is_ctx)
    def _():
        _outffn_rows(c_ref, *shared, co_ref)


def _outffn_rows(x_ref, mod_ref, po_ref, u_ref, y0_ref, y1_ref, ao_ref, d_ref, gw_ref, gb_ref,
                 wo_ref, g_ref, wg_ref, wu_ref, wd_ref, o_ref):
    o1 = POOL_WIDTH
    o2 = POOL_WIDTH + SSM_WIDTH
    d_ff = wg_ref.shape[1]

    def mixer(r):
        y = d_ref[...] * u_ref[r, :] + y0_ref[r, :] + y1_ref[r, :]
        ge = _gelu_tanh(y)
        so = ge * _sigmoid(_dot(ge.astype(bf16), gw_ref[...]) + gb_ref[...])
        mix = (_dot(po_ref[r, :], wo_ref[0:o1, :]) + _dot(so.astype(bf16), wo_ref[o1:o2, :])
               + _dot(ao_ref[r, :], wo_ref[o2:, :]))
        x1 = x_ref[r, :] + mod_ref[2:3, :] * mix
        return x1, _rmsnorm_mod(x1, g_ref[...], mod_ref[4:5, :], mod_ref[3:4, :]).astype(bf16)

    def ffn(r, x1, h):
        acc = jnp.zeros_like(x1)
        for c0 in range(0, d_ff, FF_CHUNK):
            c1 = min(c0 + FF_CHUNK, d_ff)
            gate = _dot(h, wg_ref[:, c0:c1])
            up = _dot(h, wu_ref[:, c0:c1])
            acc = acc + _dot((_silu(gate) * up).astype(bf16), wd_ref[c0:c1, :])
        o_ref[r, :] = x1 + mod_ref[5:6, :] * acc

    _chained(x_ref.shape[0], mixer, ffn)


def _outffn(layer, x, ctx, mod, po, u, y, ao, d_skip, glu_w, glu_b, w_out, g, w_gate, w_up, w_down):
    B, L, D = x.shape
    blk = LATENT_BLOCK
    nj = L // blk
    with_ctx = ctx is not None
    d_ff = w_gate.shape[2]
    tok = lambda w: pl.BlockSpec((None, blk, w), lambda b, j: (b, j, 0))
    ydir = lambda d: pl.BlockSpec((None, None, blk, SSM_WIDTH), lambda b, j: (d, b, j, 0))
    x_spec = pl.BlockSpec((None, blk, D), lambda b, j: (b, jnp.minimum(j, nj - 1), 0))
    c_specs, c_shapes, c_args = [], [], []
    if with_ctx:
        c_specs = [pl.BlockSpec((None, ctx.shape[1], D), lambda b, j: (b, 0, 0))]
        c_shapes = [jax.ShapeDtypeStruct(ctx.shape, f32)]
        c_args = [ctx]
    outs = pl.pallas_call(
        functools.partial(_outffn_kernel, with_ctx=with_ctx),
        out_shape=[jax.ShapeDtypeStruct((B, L, D), f32)] + c_shapes,
        grid=(B, nj + len(c_args)),
        in_specs=[x_spec] + c_specs + [
            pl.BlockSpec((None, None, N_MOD, D), lambda b, j: (layer, _mod_row(j, nj, b, B), 0, 0)),
            tok(POOL_WIDTH),
            tok(SSM_WIDTH),
            ydir(0),
            ydir(1),
            tok(ATTN_WIDTH),
            _layer_spec((1, SSM_WIDTH), layer),
            _layer_spec((SSM_WIDTH, SSM_WIDTH), layer),
            _layer_spec((1, SSM_WIDTH), layer),
            _layer_spec((D, D), layer),
            _layer_spec((1, D), layer),
            _layer_spec((D, d_ff), layer),
            _layer_spec((D, d_ff), layer),
            _layer_spec((d_ff, D), layer),
        ],
        out_specs=[x_spec] + c_specs,
        compiler_params=_params("arbitrary", "arbitrary"),
        name="outproj_ffn",
    )(x, *c_args, mod, po, u, y, y, ao, d_skip, glu_w, glu_b, w_out, g, w_gate, w_up, w_down)
    return (outs[0], outs[1]) if with_ctx else (outs[0], None)


def _rope_tables(n_lat, n_ctx):
    pos = jnp.arange(n_lat)
    row = (pos // GRID_W).astype(f32)
    col = (pos % GRID_W).astype(f32)
    inv = jnp.power(ROPE_BASE, -jnp.arange(ROPE_FREQS, dtype=f32) / ROPE_FREQS)
    lane = jnp.arange(LANES)
    dim = lane % HEAD_DIM
    freq = inv[dim % ROPE_FREQS]
    ang = jnp.where((dim // (2 * ROPE_FREQS)) == 0, row[:, None], col[:, None]) * freq[None, :]
    sign = jnp.where((dim % (2 * ROPE_FREQS)) < ROPE_FREQS, -1.0, 1.0).astype(f32)
    cos = jnp.concatenate([jnp.cos(ang), jnp.ones((n_ctx, LANES), f32)], axis=0)
    sin = jnp.concatenate([jnp.sin(ang) * sign, jnp.zeros((n_ctx, LANES), f32)], axis=0)
    return cos, sin


def _block_diag(blocks):
    *lead, g, r, c = blocks.shape
    on_diag = (jnp.arange(g)[:, None, None, None] == jnp.arange(g)[None, None, :, None])
    spread = jnp.where(on_diag, blocks[..., :, :, None, :], jnp.zeros((), blocks.dtype))
    return spread.reshape(*lead, g * r, g * c)


def _ssm_matrices(a_re, a_im, log_dt, b_re, b_im, c_re, c_im, n_batch):
    lr = jnp.minimum(a_re.astype(f32), -1e-4)
    li = a_im.astype(f32)
    dt = jnp.exp(log_dt.astype(f32))[..., None]
    mag = jnp.exp(lr * dt)
    ar = mag * jnp.cos(li * dt)
    ai = mag * jnp.sin(li * dt)
    den = lr * lr + li * li
    qr = ((ar - 1.0) * lr + ai * li) / den
    qi = (ai * lr - (ar - 1.0) * li) / den
    bbr = qr[..., None] * b_re - qi[..., None] * b_im
    bbi = qr[..., None] * b_im + qi[..., None] * b_re
    bd = lambda a: _block_diag(jnp.swapaxes(a, -1, -2).astype(bf16))
    bmat = jnp.concatenate([bd(bbr), bd(bbi)], axis=-1)
    cmat = jnp.concatenate([bd(c_re), bd(-c_im)], axis=-2)
    shape = a_re.shape[:2] + (n_batch, SSM_LANES)
    a_re_b = jnp.broadcast_to(ar.reshape(a_re.shape[:2] + (1, SSM_LANES)), shape)
    a_im_b = jnp.broadcast_to(ai.reshape(a_re.shape[:2] + (1, SSM_LANES)), shape)
    return bmat.astype(bf16), cmat.astype(bf16), a_re_b, a_im_b


def kernel(x, c, ctx, c_ctx, w_mod, b_mod, norm_mix, norm_ffn, w_in, w_out, pool_w, pool_scale, ssm_a_re, ssm_a_im, ssm_log_dt, ssm_b_re, ssm_b_im, ssm_c_re, ssm_c_im, ssm_d, ssm_glu_w, ssm_glu_b, q_norm, k_norm, attn_sink, ffn_w_gate, ffn_w_up, ffn_w_down):
    B, L, D = x.shape
    Lc = ctx.shape[1]
    NT = L + Lc
    depth = w_mod.shape[0]
    assert Lc == CHAIN_ROWS and L % LATENT_BLOCK == 0 and L % GRID_W == 0 and L >= 3 * WINDOW
    assert B == SUBLANES and w_in.shape[2] == POOL_WIDTH + SSM_WIDTH + ATTN_WIDTH + 2 * KV_WIDTH
    assert w_mod.shape[2] == N_MOD * D and Lc % (2 * SCAN_STEPS) == 0 and L % (2 * SCAN_STEPS) == 0

    c16 = jnp.concatenate([c, c_ctx[None, :], jnp.zeros((16 - B - 1, D), f32)], axis=0)
    mod = _modulation(c16, w_mod, b_mod).reshape(depth, 16, N_MOD, D)

    cos_t, sin_t = _rope_tables(L, LATENT_BLOCK)
    head = jnp.arange(ATTN_WIDTH) // HEAD_DIM
    ones = jnp.where(head[:, None] == head[None, :], 1.0 / HEAD_DIM, 0.0).astype(bf16)
    qn = (jnp.tile(q_norm, (1, N_HEADS)) * (HEAD_DIM ** -0.5 * LOG2_E)).reshape(depth, 1, ATTN_WIDTH)
    kn = jnp.tile(k_norm, (1, N_KV_HEADS)).reshape(depth, 1, KV_WIDTH)
    bmat, cmat, a_re, a_im = _ssm_matrices(ssm_a_re, ssm_a_im, ssm_log_dt, ssm_b_re, ssm_b_im,
                                           ssm_c_re, ssm_c_im, B)
    pool_bd = _block_diag(pool_w).astype(bf16)
    w_in_b, w_out_b, glu_w_b = w_in.astype(bf16), w_out.astype(bf16), ssm_glu_w.astype(bf16)
    w_gate_b, w_up_b, w_down_b = ffn_w_gate.astype(bf16), ffn_w_up.astype(bf16), ffn_w_down.astype(bf16)
    row = lambda a: a.reshape(depth, 1, a.shape[-1])

    for l in range(depth):
        pool_u, u, q, k2, v2 = _inproj(l, x, ctx, mod, row(norm_mix), w_in_b, cos_t, sin_t, qn, kn,
                                       ones)
        po = _pool(l, pool_u, pool_bd, row(pool_scale), L)
        y = _ssm(l, u, bmat, cmat, a_re, a_im, L)
        ao = _attention(l, attn_sink, q, k2, v2, L)
        x, ctx = _outffn(l, x, ctx if l < depth - 1 else None, mod, po, u, y, ao, row(ssm_d),
                         glu_w_b, row(ssm_glu_b), w_out_b, row(norm_ffn), w_gate_b, w_up_b, w_down_b)
    return x
```

```python
import functools
import math

import jax
import jax.numpy as jnp
from jax import lax
from jax.experimental import pallas as pl
from jax.experimental.pallas import tpu as pltpu

f32 = jnp.float32
bf16 = jnp.bfloat16

GRID_W = 64
POOL_WINDOWS = (2, 4, 8, 16)
POOL_GROUP = 64
POOL_WIDTH = POOL_GROUP * len(POOL_WINDOWS)
SSM_GROUP_CH = 16
SSM_GROUPS = 16
SSM_WIDTH = SSM_GROUP_CH * SSM_GROUPS
SSM_STATE = 64
SSM_LANES = SSM_GROUPS * SSM_STATE
HEAD_DIM = 64
N_HEADS = 8
N_KV_HEADS = 2
GROUP = N_HEADS // N_KV_HEADS
ATTN_WIDTH = N_HEADS * HEAD_DIM
KV_WIDTH = N_KV_HEADS * HEAD_DIM
WINDOW = 128
ROPE_BASE = 10000.0
ROPE_FREQS = HEAD_DIM // 4
N_MOD = 6
EPS = 1e-6
LOG2_E = math.log2(math.e)

LANES = 128
SUBLANES = 8
MXU_TILE = 256
CHAIN_ROWS = 256
LATENT_BLOCK = 4 * CHAIN_ROWS
Q_TILE = 128
Q_BLOCK = 2 * Q_TILE
SCAN_STEPS = 64
SCAN_PITCH = SCAN_STEPS + 4
FF_CHUNK = 6 * MXU_TILE
VMEM_LIMIT_BYTES = 56 * 1024 * 1024


def _sigmoid(x):
    return 1.0 / (1.0 + jnp.exp(-x))


def _silu(x):
    return x * _sigmoid(x)


def _gelu_tanh(x):
    return 0.5 * x * (1.0 + jnp.tanh(math.sqrt(2.0 / math.pi) * (x + 0.044715 * (x * x * x))))


def _dot(a, b):
    return jnp.dot(a, b, preferred_element_type=f32)


def _dot_nt(a, b):
    return lax.dot_general(a, b, (((1,), (1,)), ((), ())), preferred_element_type=f32)


def _const_spec(shape):
    nd = len(shape)
    return pl.BlockSpec(shape, lambda *_: (0,) * nd, pipeline_mode=pl.Buffered(1))


def _layer_spec(shape, layer):
    nd = len(shape)
    return pl.BlockSpec((None,) + tuple(shape), lambda *_: (layer,) + (0,) * nd,
                        pipeline_mode=pl.Buffered(1))


def _params(*sem):
    return pltpu.CompilerParams(dimension_semantics=sem, vmem_limit_bytes=VMEM_LIMIT_BYTES)


def _chained(n_rows, head, tail):
    rows = [slice(r, r + CHAIN_ROWS) for r in range(0, n_rows, CHAIN_ROWS)]
    nxt = head(rows[0])
    for i, r in enumerate(rows):
        cur = nxt
        if i + 1 < len(rows):
            nxt = head(rows[i + 1])
        tail(r, *cur)


def _mod_kernel(c_ref, w_ref, b_ref, o_ref):
    a = _silu(c_ref[...]).astype(bf16)
    o_ref[...] = _dot(a, w_ref[...].astype(bf16)) + b_ref[...]


def _modulation(c16, w_mod, b_mod):
    depth, d, n = w_mod.shape
    tn = 1536
    return pl.pallas_call(
        _mod_kernel,
        out_shape=jax.ShapeDtypeStruct((depth, 16, n), f32),
        grid=(depth, n // tn),
        in_specs=[
            pl.BlockSpec((16, d), lambda l, j: (0, 0)),
            pl.BlockSpec((None, d, tn), lambda l, j: (l, 0, j)),
            pl.BlockSpec((None, 1, tn), lambda l, j: (l, 0, j)),
        ],
        out_specs=pl.BlockSpec((None, 16, tn), lambda l, j: (l, 0, j)),
        compiler_params=_params("arbitrary", "arbitrary"),
        name="modulation",
    )(c16, w_mod, b_mod.reshape(depth, 1, n))


def _rmsnorm_mod(x, g, scale, shift):
    y = x * lax.rsqrt(jnp.mean(x * x, axis=-1, keepdims=True) + EPS) * g
    return y * (1.0 + scale) + shift


def _rope(x, cos, sin_signed):
    lane = lax.broadcasted_iota(jnp.int32, x.shape, 1)
    first = (lane % 32) < 16
    partner = jnp.where(first, pltpu.roll(x, LANES - 16, 1), pltpu.roll(x, 16, 1))
    return x * cos + partner * sin_signed


def _inproj_kernel(x_ref, c_ref, mod_ref, g_ref, w_ref, cos_ref, sin_ref, qn_ref, kn_ref, ones_ref,
                   pool_ref, ssm_ref, q_ref, k2_ref, v2_ref):
    is_ctx = pl.program_id(0) == pl.num_programs(0) - 1

    @pl.when(jnp.logical_not(is_ctx))
    def _():
        _inproj_rows(x_ref, mod_ref, g_ref, w_ref, cos_ref, sin_ref, qn_ref, kn_ref, ones_ref,
                     pool_ref, ssm_ref, q_ref, k2_ref, v2_ref)

    @pl.when(is_ctx)
    def _():
        _inproj_rows(c_ref, mod_ref, g_ref, w_ref, cos_ref, sin_ref, qn_ref, kn_ref, ones_ref,
                     pool_ref, ssm_ref, q_ref, k2_ref, v2_ref)


def _inproj_rows(x_ref, mod_ref, g_ref, w_ref, cos_ref, sin_ref, qn_ref, kn_ref, ones_ref,
                 pool_ref, ssm_ref, q_ref, k2_ref, v2_ref):
    def norm(r):
        return (_rmsnorm_mod(x_ref[r, :], g_ref[...], mod_ref[1:2, :], mod_ref[0:1, :]).astype(bf16),)

    def project(r, h):
        proj = _dot(h, w_ref[...])
        pool_ref[r, :] = proj[:, 0:POOL_WIDTH]
        ssm_ref[r, :] = proj[:, POOL_WIDTH:POOL_WIDTH + SSM_WIDTH]
        o = POOL_WIDTH + SSM_WIDTH
        q = proj[:, o:o + ATTN_WIDTH]
        k = proj[:, o + ATTN_WIDTH:o + ATTN_WIDTH + KV_WIDTH]
        v = proj[:, o + ATTN_WIDTH + KV_WIDTH:o + ATTN_WIDTH + 2 * KV_WIDTH]
        cos = cos_ref[r, :]
        sin = sin_ref[r, :]
        q_ms = _dot((q * q).astype(bf16), ones_ref[...])
        qn = q * lax.rsqrt(q_ms + EPS) * qn_ref[...]
        for cblk in range(ATTN_WIDTH // LANES):
            sl = slice(cblk * LANES, (cblk + 1) * LANES)
            q_ref[r, sl] = _rope(qn[:, sl], cos, sin).astype(bf16)
        k_ms = _dot((k * k).astype(bf16), ones_ref[0:KV_WIDTH, 0:KV_WIDTH])
        kr = _rope(k * lax.rsqrt(k_ms + EPS) * kn_ref[...], cos, sin)
        lane = lax.broadcasted_iota(jnp.int32, kr.shape, 1)
        left = lane < HEAD_DIM
        ksw = pltpu.roll(kr, HEAD_DIM, 1)
        k2_ref[r, 0:LANES] = jnp.where(left, kr, ksw).astype(bf16)
        k2_ref[r, LANES:2 * LANES] = jnp.where(left, ksw, kr).astype(bf16)
        vsw = pltpu.roll(v, HEAD_DIM, 1)
        ones_col = jnp.where(lane == HEAD_DIM, 1.0, 0.0)
        v2_ref[r, 0:LANES] = jnp.where(left, v, ones_col).astype(bf16)
        v2_ref[r, LANES:2 * LANES] = jnp.where(left, vsw, ones_col).astype(bf16)

    _chained(x_ref.shape[0], norm, project)


def _mod_row(j, n_lat_steps, b, n_batch):
    return jnp.where(j == n_lat_steps, n_batch, b)


def _inproj(layer, x, ctx, mod, g, w_in, cos_t, sin_t, qn, kn, ones):
    B, L, D = x.shape
    Lc = ctx.shape[1]
    blk = LATENT_BLOCK
    nj = L // blk
    win = w_in.shape[2]
    widths = (POOL_WIDTH, SSM_WIDTH, ATTN_WIDTH, 2 * KV_WIDTH, 2 * KV_WIDTH)
    dtypes = (f32, f32, bf16, bf16, bf16)
    lat = lambda j, b: jnp.where(j == nj, B - 1, b)
    con = lambda j, b: jnp.where(j == nj, b, 0)
    return pl.pallas_call(
        _inproj_kernel,
        out_shape=tuple(jax.ShapeDtypeStruct((B, L + Lc, w), dt) for w, dt in zip(widths, dtypes)),
        grid=(nj + 1, B),
        in_specs=[
            pl.BlockSpec((None, blk, D), lambda j, b: (lat(j, b), jnp.minimum(j, nj - 1), 0)),
            pl.BlockSpec((None, Lc, D), lambda j, b: (con(j, b), 0, 0)),
            pl.BlockSpec((None, None, N_MOD, D), lambda j, b: (layer, _mod_row(j, nj, b, B), 0, 0)),
            _layer_spec((1, D), layer),
            _layer_spec((D, win), layer),
            pl.BlockSpec((blk, LANES), lambda j, b: (j, 0)),
            pl.BlockSpec((blk, LANES), lambda j, b: (j, 0)),
            _layer_spec((1, ATTN_WIDTH), layer),
            _layer_spec((1, KV_WIDTH), layer),
            _const_spec((ATTN_WIDTH, ATTN_WIDTH)),
        ],
        out_specs=tuple(pl.BlockSpec((None, blk, w), lambda j, b: (b, j, 0)) for w in widths),
        compiler_params=_params("arbitrary", "arbitrary"),
        name="inproj",
    )(x, ctx, mod, g, w_in, cos_t, sin_t, qn, kn, ones)


def _shift_rows(x, s, row):
    n = x.shape[0]
    if s > 0:
        return jnp.where(row >= s, pltpu.roll(x, s, 0), 0.0)
    return jnp.where(row < n + s, pltpu.roll(x, n + s, 0), 0.0)


def _window_sums(u, half_a, half_b, lane_split):
    row = lax.broadcasted_iota(jnp.int32, u.shape, 0)
    lane = lax.broadcasted_iota(jnp.int32, u.shape, 1)
    bwd = u
    fwd = u
    k = 1
    out_a = None
    while True:
        if k == half_a:
            out_a = _shift_rows(bwd, 1, row) + fwd
        if k == half_b:
            out_b = _shift_rows(bwd, 1, row) + fwd
            break
        bwd = bwd + _shift_rows(bwd, k, row)
        fwd = fwd + _shift_rows(fwd, -k, row)
        k *= 2
    n = u.shape[0]
    half = jnp.where(lane < lane_split, half_a, half_b)
    cnt = jnp.minimum(row + half, n) - jnp.maximum(row - half, 0)
    return jnp.where(lane < lane_split, out_a, out_b) / cnt.astype(f32)


def _pool_segment(u_ref, w_ref, scale_ref, o_ref, r0, n):
    parts = []
    for blk in range(POOL_WIDTH // LANES):
        u = u_ref[r0:r0 + n, blk * LANES:(blk + 1) * LANES]
        wa, wb = POOL_WINDOWS[2 * blk], POOL_WINDOWS[2 * blk + 1]
        mean = _window_sums(u, wa // 2, wb // 2, POOL_GROUP)
        parts.append((mean - u).astype(bf16))
    d = jnp.concatenate(parts, axis=1)
    o_ref[r0:r0 + n, :] = (_dot(d, w_ref[...]) * scale_ref[...]).astype(bf16)


def _pool_kernel(u_ref, w_ref, scale_ref, o_ref, *, n_lat):
    nt = u_ref.shape[0]
    _pool_segment(u_ref, w_ref, scale_ref, o_ref, 0, n_lat)
    _pool_segment(u_ref, w_ref, scale_ref, o_ref, n_lat, nt - n_lat)


def _pool(layer, pool_u, w_bd, scale, n_lat):
    B, NT, W = pool_u.shape
    return pl.pallas_call(
        functools.partial(_pool_kernel, n_lat=n_lat),
        out_shape=jax.ShapeDtypeStruct((B, NT, W), bf16),
        grid=(B,),
        in_specs=[
            pl.BlockSpec((None, NT, W), lambda b: (b, 0, 0)),
            _layer_spec((W, W), layer),
            _layer_spec((1, W), layer),
        ],
        out_specs=pl.BlockSpec((None, NT, W), lambda b: (b, 0, 0)),
        compiler_params=_params("arbitrary"),
        name="pool",
    )(pool_u, w_bd, scale)


def _ssm_kernel(u_ref, bmat_ref, cmat_ref, are_ref, aim_ref, y_ref, upad_ref, state_ref,
                *bufs, n_chunks):
    g = pl.program_id(0)
    nb = u_ref.shape[0]
    ts, pitch = SCAN_STEPS, SCAN_PITCH
    n_seq = 2 * n_chunks
    nk = SSM_LANES // LANES
    bu_even, bu_odd, s_even, s_odd = bufs[0:2], bufs[2:4], bufs[4:6], bufs[6:8]

    @pl.when(g == 0)
    def _():
        for r in (upad_ref, state_ref) + tuple(bufs):
            r[...] = jnp.zeros_like(r)

    def direction(i):
        return jnp.clip(i, 0, n_seq - 1) // n_chunks

    def half_rows(i, h):
        off = jnp.where(direction(i) == 0, h, 1 - h) * ts
        return pl.ds(pl.multiple_of(off, ts), ts)

    def stage_b(i, h, bu_ref):
        rows = half_rows(i, h)
        for b in range(nb):
            upad_ref[b * pitch:b * pitch + ts, :] = u_ref[b, rows, :]
        res = _dot(upad_ref[...].astype(bf16), bmat_ref[direction(i)])
        for k in range(2 * nk):
            bu_ref[k] = res[:, k * LANES:(k + 1) * LANES]

    def stage_scan(i, bu_ref, s_ref):
        ic = jnp.clip(i, 0, n_seq - 1)
        d = ic // n_chunks
        keep = jnp.where((ic % n_chunks) == 0, 0.0, 1.0)
        a_re = are_ref[d]
        a_im = aim_ref[d]
        st_re = [state_ref[:, k * LANES:(k + 1) * LANES] * keep for k in range(nk)]
        st_im = [state_ref[:, (nk + k) * LANES:(nk + k + 1) * LANES] * keep for k in range(nk)]
        for t in range(ts):
            tt = jnp.where(d == 0, t, ts - 1 - t)
            rows = pl.ds(tt, nb, stride=pitch)
            for k in range(nk):
                ar = a_re[:, k * LANES:(k + 1) * LANES]
                ai = a_im[:, k * LANES:(k + 1) * LANES]
                n_re = ar * st_re[k] - ai * st_im[k] + bu_ref[k, rows, :]
                n_im = ar * st_im[k] + ai * st_re[k] + bu_ref[nk + k, rows, :]
                s_ref[k, rows, :] = n_re
                s_ref[nk + k, rows, :] = n_im
                st_re[k], st_im[k] = n_re, n_im
        for k in range(nk):
            state_ref[:, k * LANES:(k + 1) * LANES] = st_re[k]
            state_ref[:, (nk + k) * LANES:(nk + k + 1) * LANES] = st_im[k]

    def stage_c(i, h, s_ref):
        s = jnp.concatenate([s_ref[k] for k in range(2 * nk)], axis=1).astype(bf16)
        yv = _dot(s, cmat_ref[direction(i)])
        rows = half_rows(i, h)
        for b in range(nb):
            y_ref[b, rows, :] = yv[b * pitch:b * pitch + ts, :]

    def step(bu_write, bu_read, s_write, s_read):
        for h in range(2):
            stage_scan(2 * g - 2 + h, bu_read[h], s_write[h])
            stage_b(2 * g + h, h, bu_write[h])
            stage_c(2 * g - 4 + h, h, s_read[h])

    @pl.when(g % 2 == 0)
    def _():
        step(bu_even, bu_odd, s_even, s_odd)

    @pl.when(g % 2 == 1)
    def _():
        step(bu_odd, bu_even, s_odd, s_even)


def _ssm(layer, u, bmat, cmat, a_re, a_im, n_lat):
    B, NT, W = u.shape
    ts = SCAN_STEPS
    n_chunks = NT // ts
    n_blocks = n_chunks // 2
    nx = n_lat // (2 * ts)
    nc = n_blocks - nx

    def block(step):
        step = jnp.clip(step, 0, 2 * n_blocks - 1)
        d = step // n_blocks
        j = step % n_blocks
        fwd = jnp.where(j < nc, nx + j, j - nc)
        back = jnp.where(j < nc, n_blocks - 1 - j, nx - 1 - (j - nc))
        return d, jnp.where(d == 0, fwd, back)

    rows = B * SCAN_PITCH
    slab = pltpu.VMEM((2 * SSM_LANES // LANES, rows, LANES), f32)
    return pl.pallas_call(
        functools.partial(_ssm_kernel, n_chunks=n_chunks),
        out_shape=jax.ShapeDtypeStruct((2, B, NT, W), f32),
        grid=(2 * n_blocks + 2,),
        in_specs=[
            pl.BlockSpec((B, 2 * ts, W), lambda g: (0, block(g)[1], 0)),
            _layer_spec((2, W, 2 * SSM_LANES), layer),
            _layer_spec((2, 2 * SSM_LANES, W), layer),
            _layer_spec((2, B, SSM_LANES), layer),
            _layer_spec((2, B, SSM_LANES), layer),
        ],
        out_specs=pl.BlockSpec((None, B, 2 * ts, W),
                               lambda g: (block(g - 2)[0], 0, block(g - 2)[1], 0)),
        scratch_shapes=[pltpu.VMEM((rows, W), f32), pltpu.VMEM((B, 2 * SSM_LANES), f32)]
        + [slab] * 8,
        compiler_params=_params("arbitrary"),
        name="ssm_scan",
    )(u, bmat, cmat, a_re, a_im)


def _attend_group(q_pairs, keys, vals, masks, sinks):
    nq = q_pairs[0].shape[0]
    lane = lax.broadcasted_iota(jnp.int32, (nq, LANES), 1)
    low = lane < HEAD_DIM
    zero = jnp.zeros((nq, LANES), bf16)
    lhs = jnp.concatenate([jnp.where(keep, qp, zero) for qp in q_pairs for keep in (low, ~low)],
                          axis=0)
    scores = [_dot_nt(lhs, kk) for kk in keys]
    scores = [s if mk is None else s + mk for s, mk in zip(scores, masks)]
    row = lax.broadcasted_iota(jnp.int32, (4 * nq, 1), 0)
    sink = jnp.where(row < 2 * nq, jnp.where(row < nq, sinks[0], sinks[1]),
                     jnp.where(row < 3 * nq, sinks[2], sinks[3])) * LOG2_E
    cols = [s[:, c:c + LANES] for s in scores for c in range(0, s.shape[1], LANES)]
    m = jnp.maximum(sink, jnp.max(functools.reduce(jnp.maximum, cols), axis=-1, keepdims=True))
    out = jnp.zeros((4 * nq, LANES), f32)
    for s, vv in zip(scores, vals):
        out = out + _dot(jnp.exp2(s - m).astype(bf16), vv)
    out = out / (out[:, HEAD_DIM:HEAD_DIM + 1] + jnp.exp2(sink - m))
    blocks = [out[h * nq:(h + 1) * nq] for h in range(4)]
    return [jnp.where(low, blocks[2 * p], pltpu.roll(blocks[2 * p + 1], HEAD_DIM, 1))
            for p in range(2)]


def _attn_kernel(sink_ref, q_ref, k2_ref, v2_ref, o_ref, *, n_lat):
    i = pl.program_id(1)
    nq = Q_TILE
    n_sub = q_ref.shape[0] // nq
    n_ctx = k2_ref.shape[0] - n_lat
    n_lat_steps = n_lat // (n_sub * nq)
    span = 3 * WINDOW

    def run(rq, keys_rows, masks):
        for hk in range(N_KV_HEADS):
            keys = [k2_ref[r, hk * LANES:(hk + 1) * LANES] for r in keys_rows]
            vals = [v2_ref[r, hk * LANES:(hk + 1) * LANES] for r in keys_rows]
            c0 = hk * GROUP * HEAD_DIM
            outs = _attend_group([q_ref[rq, c0:c0 + LANES], q_ref[rq, c0 + LANES:c0 + 2 * LANES]],
                                 keys, vals, masks, [sink_ref[hk * GROUP + h] for h in range(GROUP)])
            for p, o in enumerate(outs):
                o_ref[rq, c0 + p * LANES:c0 + (p + 1) * LANES] = o.astype(bf16)

    ctx_rows = pl.ds(n_lat, n_ctx)

    @pl.when(i >= n_lat_steps)
    def _():
        for sub in range(n_sub):
            run(slice(sub * nq, (sub + 1) * nq), [ctx_rows], [None])

    @pl.when(i < n_lat_steps)
    def _():
        for sub in range(n_sub):
            q0 = (i * n_sub + sub) * nq
            start = jnp.clip(q0 - WINDOW, 0, n_lat - span)
            q_pos = q0 + lax.broadcasted_iota(jnp.int32, (GROUP * nq, span), 0) % nq
            k_pos = start + lax.broadcasted_iota(jnp.int32, (GROUP * nq, span), 1)
            bias = jnp.where(jnp.abs(k_pos - q_pos) <= WINDOW, 0.0, -jnp.inf)
            win_rows = pl.ds(pl.multiple_of(start, WINDOW), span)
            run(slice(sub * nq, (sub + 1) * nq), [win_rows, ctx_rows], [bias, None])


def _attention(layer, sink, q, k2, v2, n_lat):
    B, NT, _ = q.shape
    return pl.pallas_call(
        functools.partial(_attn_kernel, n_lat=n_lat),
        out_shape=jax.ShapeDtypeStruct((B, NT, ATTN_WIDTH), bf16),
        grid=(B, NT // Q_BLOCK),
        in_specs=[
            pl.BlockSpec(memory_space=pltpu.SMEM),
            pl.BlockSpec((None, Q_BLOCK, ATTN_WIDTH), lambda b, i: (b, i, 0)),
            pl.BlockSpec((None, NT, 2 * KV_WIDTH), lambda b, i: (b, 0, 0)),
            pl.BlockSpec((None, NT, 2 * KV_WIDTH), lambda b, i: (b, 0, 0)),
        ],
        out_specs=pl.BlockSpec((None, Q_BLOCK, ATTN_WIDTH), lambda b, i: (b, i, 0)),
        compiler_params=_params("arbitrary", "arbitrary"),
        name="attention",
    )(sink[layer], q, k2, v2)


def _outffn_kernel(*refs, with_ctx):
    if not with_ctx:
        _outffn_rows(refs[0], *refs[1:-1], refs[-1])
        return
    x_ref, c_ref, shared, xo_ref, co_ref = refs[0], refs[1], refs[2:-2], refs[-2], refs[-1]
    is_ctx = pl.program_id(1) == pl.num_programs(1) - 1

    @pl.when(jnp.logical_not(is_ctx))
    def _():
        _outffn_rows(x_ref, *shared, xo_ref)

    @pl.when(is_ctx)
    def _():
        _outffn_rows(c_ref, *shared, co_ref)


def _outffn_rows(x_ref, mod_ref, po_ref, u_ref, y0_ref, y1_ref, ao_ref, d_ref, gw_ref, gb_ref,
                 wo_ref, g_ref, wg_ref, wu_ref, wd_ref, o_ref):
    o1 = POOL_WIDTH
    o2 = POOL_WIDTH + SSM_WIDTH
    d_ff = wg_ref.shape[1]

    def mixer(r):
        y = d_ref[...] * u_ref[r, :] + y0_ref[r, :] + y1_ref[r, :]
        ge = _gelu_tanh(y)
        so = ge * _sigmoid(_dot(ge.astype(bf16), gw_ref[...]) + gb_ref[...])
        mix = (_dot(po_ref[r, :], wo_ref[0:o1, :]) + _dot(so.astype(bf16), wo_ref[o1:o2, :])
               + _dot(ao_ref[r, :], wo_ref[o2:, :]))
        x1 = x_ref[r, :] + mod_ref[2:3, :] * mix
        return x1, _rmsnorm_mod(x1, g_ref[...], mod_ref[4:5, :], mod_ref[3:4, :]).astype(bf16)

    def ffn(r, x1, h):
        acc = jnp.zeros_like(x1)
        for c0 in range(0, d_ff, FF_CHUNK):
            c1 = min(c0 + FF_CHUNK, d_ff)
            gate = _dot(h, wg_ref[:, c0:c1])
            up = _dot(h, wu_ref[:, c0:c1])
            acc = acc + _dot((_silu(gate) * up).astype(bf16), wd_ref[c0:c1, :])
        o_ref[r, :] = x1 + mod_ref[5:6, :] * acc

    _chained(x_ref.shape[0], mixer, ffn)


def _outffn(layer, x, ctx, mod, po, u, y, ao, d_skip, glu_w, glu_b, w_out, g, w_gate, w_up, w_down):
    B, L, D = x.shape
    blk = LATENT_BLOCK
    nj = L // blk
    with_ctx = ctx is not None
    d_ff = w_gate.shape[2]
    tok = lambda w: pl.BlockSpec((None, blk, w), lambda b, j: (b, j, 0))
    ydir = lambda d: pl.BlockSpec((None, None, blk, SSM_WIDTH), lambda b, j: (d, b, j, 0))
    x_spec = pl.BlockSpec((None, blk, D), lambda b, j: (b, jnp.minimum(j, nj - 1), 0))
    c_specs, c_shapes, c_args = [], [], []
    if with_ctx:
        c_specs = [pl.BlockSpec((None, ctx.shape[1], D), lambda b, j: (b, 0, 0))]
        c_shapes = [jax.ShapeDtypeStruct(ctx.shape, f32)]
        c_args = [ctx]
    outs = pl.pallas_call(
        functools.partial(_outffn_kernel, with_ctx=with_ctx),
        out_shape=[jax.ShapeDtypeStruct((B, L, D), f32)] + c_shapes,
        grid=(B, nj + len(c_args)),
        in_specs=[x_spec] + c_specs + [
            pl.BlockSpec((None, None, N_MOD, D), lambda b, j: (layer, _mod_row(j, nj, b, B), 0, 0)),
            tok(POOL_WIDTH),
            tok(SSM_WIDTH),
            ydir(0),
            ydir(1),
            tok(ATTN_WIDTH),
            _layer_spec((1, SSM_WIDTH), layer),
            _layer_spec((SSM_WIDTH, SSM_WIDTH), layer),
            _layer_spec((1, SSM_WIDTH), layer),
            _layer_spec((D, D), layer),
            _layer_spec((1, D), layer),
            _layer_spec((D, d_ff), layer),
            _layer_spec((D, d_ff), layer),
            _layer_spec((d_ff, D), layer),
        ],
        out_specs=[x_spec] + c_specs,
        compiler_params=_params("arbitrary", "arbitrary"),
        name="outproj_ffn",
    )(x, *c_args, mod, po, u, y, y, ao, d_skip, glu_w, glu_b, w_out, g, w_gate, w_up, w_down)
    return (outs[0], outs[1]) if with_ctx else (outs[0], None)


def _rope_tables(n_lat, n_ctx):
    pos = jnp.arange(n_lat)
    row = (pos // GRID_W).astype(f32)
    col = (pos % GRID_W).astype(f32)
    inv = jnp.power(ROPE_BASE, -jnp.arange(ROPE_FREQS, dtype=f32) / ROPE_FREQS)
    lane = jnp.arange(LANES)
    dim = lane % HEAD_DIM
    freq = inv[dim % ROPE_FREQS]
    ang = jnp.where((dim // (2 * ROPE_FREQS)) == 0, row[:, None], col[:, None]) * freq[None, :]
    sign = jnp.where((dim % (2 * ROPE_FREQS)) < ROPE_FREQS, -1.0, 1.0).astype(f32)
    cos = jnp.concatenate([jnp.cos(ang), jnp.ones((n_ctx, LANES), f32)], axis=0)
    sin = jnp.concatenate([jnp.sin(ang) * sign, jnp.zeros((n_ctx, LANES), f32)], axis=0)
    return cos, sin


def _block_diag(blocks):
    *lead, g, r, c = blocks.shape
    on_diag = (jnp.arange(g)[:, None, None, None] == jnp.arange(g)[None, None, :, None])
    spread = jnp.where(on_diag, blocks[..., :, :, None, :], jnp.zeros((), blocks.dtype))
    return spread.reshape(*lead, g * r, g * c)


def _ssm_matrices(a_re, a_im, log_dt, b_re, b_im, c_re, c_im, n_batch):
    lr = jnp.minimum(a_re.astype(f32), -1e-4)
    li = a_im.astype(f32)
    dt = jnp.exp(log_dt.astype(f32))[..., None]
    mag = jnp.exp(lr * dt)
    ar = mag * jnp.cos(li * dt)
    ai = mag * jnp.sin(li * dt)
    den = lr * lr + li * li
    qr = ((ar - 1.0) * lr + ai * li) / den
    qi = (ai * lr - (ar - 1.0) * li) / den
    bbr = qr[..., None] * b_re - qi[..., None] * b_im
    bbi = qr[..., None] * b_im + qi[..., None] * b_re
    bd = lambda a: _block_diag(jnp.swapaxes(a, -1, -2).astype(bf16))
    bmat = jnp.concatenate([bd(bbr), bd(bbi)], axis=-1)
    cmat = jnp.concatenate([bd(c_re), bd(-c_im)], axis=-2)
    shape = a_re.shape[:2] + (n_batch, SSM_LANES)
    a_re_b = jnp.broadcast_to(ar.reshape(a_re.shape[:2] + (1, SSM_LANES)), shape)
    a_im_b = jnp.broadcast_to(ai.reshape(a_re.shape[:2] + (1, SSM_LANES)), shape)
    return bmat.astype(bf16), cmat.astype(bf16), a_re_b, a_im_b


def kernel(x, c, ctx, c_ctx, w_mod, b_mod, norm_mix, norm_ffn, w_in, w_out, pool_w, pool_scale, ssm_a_re, ssm_a_im, ssm_log_dt, ssm_b_re, ssm_b_im, ssm_c_re, ssm_c_im, ssm_d, ssm_glu_w, ssm_glu_b, q_norm, k_norm, attn_sink, ffn_w_gate, ffn_w_up, ffn_w_down):
    B, L, D = x.shape
    Lc = ctx.shape[1]
    NT = L + Lc
    depth = w_mod.shape[0]
    assert Lc == CHAIN_ROWS and L % LATENT_BLOCK == 0 and L % GRID_W == 0 and L >= 3 * WINDOW
    assert B == SUBLANES and w_in.shape[2] == POOL_WIDTH + SSM_WIDTH + ATTN_WIDTH + 2 * KV_WIDTH
    assert w_mod.shape[2] == N_MOD * D and Lc % (2 * SCAN_STEPS) == 0 and L % (2 * SCAN_STEPS) == 0

    c16 = jnp.concatenate([c, c_ctx[None, :], jnp.zeros((16 - B - 1, D), f32)], axis=0)
    mod = _modulation(c16, w_mod, b_mod).reshape(depth, 16, N_MOD, D)

    cos_t, sin_t = _rope_tables(L, LATENT_BLOCK)
    head = jnp.arange(ATTN_WIDTH) // HEAD_DIM
    ones = jnp.where(head[:, None] == head[None, :], 1.0 / HEAD_DIM, 0.0).astype(bf16)
    qn = (jnp.tile(q_norm, (1, N_HEADS)) * (HEAD_DIM ** -0.5 * LOG2_E)).reshape(depth, 1, ATTN_WIDTH)
    kn = jnp.tile(k_norm, (1, N_KV_HEADS)).reshape(depth, 1, KV_WIDTH)
    bmat, cmat, a_re, a_im = _ssm_matrices(ssm_a_re, ssm_a_im, ssm_log_dt, ssm_b_re, ssm_b_im,
                                           ssm_c_re, ssm_c_im, B)
    pool_bd = _block_diag(pool_w).astype(bf16)
    w_in_b, w_out_b, glu_w_b = w_in.astype(bf16), w_out.astype(bf16), ssm_glu_w.astype(bf16)
    w_gate_b, w_up_b, w_down_b = ffn_w_gate.astype(bf16), ffn_w_up.astype(bf16), ffn_w_down.astype(bf16)
    row = lambda a: a.reshape(depth, 1, a.shape[-1])

    for l in range(depth):
        pool_u, u, q, k2, v2 = _inproj(l, x, ctx, mod, row(norm_mix), w_in_b, cos_t, sin_t, qn, kn,
                                       ones)
        po = _pool(l, pool_u, pool_bd, row(pool_scale), L)
        y = _ssm(l, u, bmat, cmat, a_re, a_im, L)
        ao = _attention(l, attn_sink, q, k2, v2, L)
        x, ctx = _outffn(l, x, ctx if l < depth - 1 else None, mod, po, u, y, ao, row(ssm_d),
                         glu_w_b, row(ssm_glu_b), w_out_b, row(norm_ffn), w_gate_b, w_up_b, w_down_b)
    return x
```

```python
import functools
import math

import jax
import jax.numpy as jnp
from jax import lax
from jax.experimental import pallas as pl
from jax.experimental.pallas import tpu as pltpu

f32 = jnp.float32
bf16 = jnp.bfloat16

GRID_W = 64
POOL_WINDOWS = (2, 4, 8, 16)
POOL_GROUP = 64
POOL_WIDTH = POOL_GROUP * len(POOL_WINDOWS)
SSM_GROUP_CH = 16
SSM_GROUPS = 16
SSM_WIDTH = SSM_GROUP_CH * SSM_GROUPS
SSM_STATE = 64
SSM_LANES = SSM_GROUPS * SSM_STATE
HEAD_DIM = 64
N_HEADS = 8
N_KV_HEADS = 2
GROUP = N_HEADS // N_KV_HEADS
ATTN_WIDTH = N_HEADS * HEAD_DIM
KV_WIDTH = N_KV_HEADS * HEAD_DIM
WINDOW = 128
ROPE_BASE = 10000.0
ROPE_FREQS = HEAD_DIM // 4
N_MOD = 6
EPS = 1e-6
LOG2_E = math.log2(math.e)

LANES = 128
SUBLANES = 8
MXU_TILE = 256
CHAIN_ROWS = 256
LATENT_BLOCK = 4 * CHAIN_ROWS
Q_TILE = 128
Q_BLOCK = 4 * Q_TILE
SCAN_STEPS = 64
SCAN_PITCH = SCAN_STEPS + 4
FF_CHUNK = 6 * MXU_TILE
VMEM_LIMIT_BYTES = 56 * 1024 * 1024


def _sigmoid(x):
    return 1.0 / (1.0 + jnp.exp(-x))


def _silu(x):
    return x * _sigmoid(x)


def _gelu_tanh(x):
    return 0.5 * x * (1.0 + jnp.tanh(math.sqrt(2.0 / math.pi) * (x + 0.044715 * (x * x * x))))


def _dot(a, b):
    return jnp.dot(a, b, preferred_element_type=f32)


def _dot_nt(a, b):
    return lax.dot_general(a, b, (((1,), (1,)), ((), ())), preferred_element_type=f32)


def _const_spec(shape):
    nd = len(shape)
    return pl.BlockSpec(shape, lambda *_: (0,) * nd, pipeline_mode=pl.Buffered(1))


def _layer_spec(shape, layer):
    nd = len(shape)
    return pl.BlockSpec((None,) + tuple(shape), lambda *_: (layer,) + (0,) * nd,
                        pipeline_mode=pl.Buffered(1))


def _params(*sem):
    return pltpu.CompilerParams(dimension_semantics=sem, vmem_limit_bytes=VMEM_LIMIT_BYTES)


def _chained(n_rows, head, tail):
    rows = [slice(r, min(r + CHAIN_ROWS, n_rows)) for r in range(0, n_rows, CHAIN_ROWS)]
    nxt = head(rows[0])
    for i, r in enumerate(rows):
        cur = nxt
        if i + 1 < len(rows):
            nxt = head(rows[i + 1])
        tail(r, *cur)


def _mod_kernel(c_ref, w_ref, b_ref, o_ref):
    a = _silu(c_ref[...]).astype(bf16)
    o_ref[...] = _dot(a, w_ref[...].astype(bf16)) + b_ref[...]


def _modulation(c16, w_mod, b_mod):
    depth, d, n = w_mod.shape
    tn = 1536
    return pl.pallas_call(
        _mod_kernel,
        out_shape=jax.ShapeDtypeStruct((depth, 16, n), f32),
        grid=(depth, n // tn),
        in_specs=[
            pl.BlockSpec((16, d), lambda l, j: (0, 0)),
            pl.BlockSpec((None, d, tn), lambda l, j: (l, 0, j)),
            pl.BlockSpec((None, 1, tn), lambda l, j: (l, 0, j)),
        ],
        out_specs=pl.BlockSpec((None, 16, tn), lambda l, j: (l, 0, j)),
        compiler_params=_params("arbitrary", "arbitrary"),
        name="modulation",
    )(c16, w_mod, b_mod.reshape(depth, 1, n))


def _rmsnorm_mod(x, g, scale, shift):
    y = x * lax.rsqrt(jnp.mean(x * x, axis=-1, keepdims=True) + EPS) * g
    return y * (1.0 + scale) + shift


def _rope(x, cos, sin_signed):
    lane = lax.broadcasted_iota(jnp.int32, x.shape, 1)
    first = (lane % 32) < 16
    partner = jnp.where(first, pltpu.roll(x, LANES - 16, 1), pltpu.roll(x, 16, 1))
    return x * cos + partner * sin_signed


def _inproj_kernel(x_ref, c_ref, mod_ref, g_ref, w_ref, cos_ref, sin_ref, qn_ref, kn_ref, ones_ref,
                   pool_ref, ssm_ref, q_ref, k2_ref, v2_ref):
    is_ctx = pl.program_id(0) == pl.num_programs(0) - 1

    @pl.when(jnp.logical_not(is_ctx))
    def _():
        _inproj_rows(x_ref, mod_ref, g_ref, w_ref, cos_ref, sin_ref, qn_ref, kn_ref, ones_ref,
                     pool_ref, ssm_ref, q_ref, k2_ref, v2_ref)

    @pl.when(is_ctx)
    def _():
        _inproj_rows(c_ref, mod_ref, g_ref, w_ref, cos_ref, sin_ref, qn_ref, kn_ref, ones_ref,
                     pool_ref, ssm_ref, q_ref, k2_ref, v2_ref)


def _inproj_rows(x_ref, mod_ref, g_ref, w_ref, cos_ref, sin_ref, qn_ref, kn_ref, ones_ref,
                 pool_ref, ssm_ref, q_ref, k2_ref, v2_ref):
    def norm(r):
        return (_rmsnorm_mod(x_ref[r, :], g_ref[...], mod_ref[1:2, :], mod_ref[0:1, :]).astype(bf16),)

    def project(r, h):
        proj = _dot(h, w_ref[...])
        pool_ref[r, :] = proj[:, 0:POOL_WIDTH]
        ssm_ref[r, :] = proj[:, POOL_WIDTH:POOL_WIDTH + SSM_WIDTH]
        o = POOL_WIDTH + SSM_WIDTH
        q = proj[:, o:o + ATTN_WIDTH]
        k = proj[:, o + ATTN_WIDTH:o + ATTN_WIDTH + KV_WIDTH]
        v = proj[:, o + ATTN_WIDTH + KV_WIDTH:o + ATTN_WIDTH + 2 * KV_WIDTH]
        cos = cos_ref[r, :]
        sin = sin_ref[r, :]
        q_ms = _dot((q * q).astype(bf16), ones_ref[...])
        qn = q * lax.rsqrt(q_ms + EPS) * qn_ref[...]
        for cblk in range(ATTN_WIDTH // LANES):
            sl = slice(cblk * LANES, (cblk + 1) * LANES)
            q_ref[r, sl] = _rope(qn[:, sl], cos, sin).astype(bf16)
        k_ms = _dot((k * k).astype(bf16), ones_ref[0:KV_WIDTH, 0:KV_WIDTH])
        kr = _rope(k * lax.rsqrt(k_ms + EPS) * kn_ref[...], cos, sin)
        lane = lax.broadcasted_iota(jnp.int32, kr.shape, 1)
        left = lane < HEAD_DIM
        ksw = pltpu.roll(kr, HEAD_DIM, 1)
        k2_ref[r, 0:LANES] = jnp.where(left, kr, ksw).astype(bf16)
        k2_ref[r, LANES:2 * LANES] = jnp.where(left, ksw, kr).astype(bf16)
        vsw = pltpu.roll(v, HEAD_DIM, 1)
        ones_col = jnp.where(lane == HEAD_DIM, 1.0, 0.0)
        v2_ref[r, 0:LANES] = jnp.where(left, v, ones_col).astype(bf16)
        v2_ref[r, LANES:2 * LANES] = jnp.where(left, vsw, ones_col).astype(bf16)

    _chained(x_ref.shape[0], norm, project)


def _mod_row(j, n_lat_steps, b, n_batch):
    return jnp.where(j == n_lat_steps, n_batch, b)


def _inproj(layer, x, ctx, mod, g, w_in, cos_t, sin_t, qn, kn, ones):
    B, L, D = x.shape
    Lc = ctx.shape[1]
    blk = LATENT_BLOCK
    nj = L // blk
    win = w_in.shape[2]
    widths = (POOL_WIDTH, SSM_WIDTH, ATTN_WIDTH, 2 * KV_WIDTH, 2 * KV_WIDTH)
    dtypes = (f32, f32, bf16, bf16, bf16)
    lat = lambda j, b: jnp.where(j == nj, B - 1, b)
    con = lambda j, b: jnp.where(j == nj, b, 0)
    return pl.pallas_call(
        _inproj_kernel,
        out_shape=tuple(jax.ShapeDtypeStruct((B, L + Lc, w), dt) for w, dt in zip(widths, dtypes)),
        grid=(nj + 1, B),
        in_specs=[
            pl.BlockSpec((None, blk, D), lambda j, b: (lat(j, b), jnp.minimum(j, nj - 1), 0)),
            pl.BlockSpec((None, Lc, D), lambda j, b: (con(j, b), 0, 0)),
            pl.BlockSpec((None, None, N_MOD, D), lambda j, b: (layer, _mod_row(j, nj, b, B), 0, 0)),
            _layer_spec((1, D), layer),
            _layer_spec((D, win), layer),
            pl.BlockSpec((blk, LANES), lambda j, b: (j, 0)),
            pl.BlockSpec((blk, LANES), lambda j, b: (j, 0)),
            _layer_spec((1, ATTN_WIDTH), layer),
            _layer_spec((1, KV_WIDTH), layer),
            _const_spec((ATTN_WIDTH, ATTN_WIDTH)),
        ],
        out_specs=tuple(pl.BlockSpec((None, blk, w), lambda j, b: (b, j, 0)) for w in widths),
        compiler_params=_params("arbitrary", "arbitrary"),
        name="inproj",
    )(x, ctx, mod, g, w_in, cos_t, sin_t, qn, kn, ones)


def _shift_rows(x, s, row):
    n = x.shape[0]
    if s > 0:
        return jnp.where(row >= s, pltpu.roll(x, s, 0), 0.0)
    return jnp.where(row < n + s, pltpu.roll(x, n + s, 0), 0.0)


def _window_sums(u, half_a, half_b, lane_split):
    row = lax.broadcasted_iota(jnp.int32, u.shape, 0)
    lane = lax.broadcasted_iota(jnp.int32, u.shape, 1)
    bwd = u
    fwd = u
    k = 1
    out_a = None
    while True:
        if k == half_a:
            out_a = _shift_rows(bwd, 1, row) + fwd
        if k == half_b:
            out_b = _shift_rows(bwd, 1, row) + fwd
            break
        bwd = bwd + _shift_rows(bwd, k, row)
        fwd = fwd + _shift_rows(fwd, -k, row)
        k *= 2
    n = u.shape[0]
    half = jnp.where(lane < lane_split, half_a, half_b)
    cnt = jnp.minimum(row + half, n) - jnp.maximum(row - half, 0)
    return jnp.where(lane < lane_split, out_a, out_b) / cnt.astype(f32)


def _pool_segment(u_ref, w_ref, scale_ref, o_ref, r0, n):
    parts = []
    for blk in range(POOL_WIDTH // LANES):
        u = u_ref[r0:r0 + n, blk * LANES:(blk + 1) * LANES]
        wa, wb = POOL_WINDOWS[2 * blk], POOL_WINDOWS[2 * blk + 1]
        mean = _window_sums(u, wa // 2, wb // 2, POOL_GROUP)
        parts.append((mean - u).astype(bf16))
    d = jnp.concatenate(parts, axis=1)
    o_ref[r0:r0 + n, :] = (_dot(d, w_ref[...]) * scale_ref[...]).astype(bf16)


def _pool_kernel(u_ref, w_ref, scale_ref, o_ref, *, n_lat):
    nt = u_ref.shape[0]
    _pool_segment(u_ref, w_ref, scale_ref, o_ref, 0, n_lat)
    _pool_segment(u_ref, w_ref, scale_ref, o_ref, n_lat, nt - n_lat)


def _pool(layer, pool_u, w_bd, scale, n_lat):
    B, NT, W = pool_u.shape
    return pl.pallas_call(
        functools.partial(_pool_kernel, n_lat=n_lat),
        out_shape=jax.ShapeDtypeStruct((B, NT, W), bf16),
        grid=(B,),
        in_specs=[
            pl.BlockSpec((None, NT, W), lambda b: (b, 0, 0)),
            _layer_spec((W, W), layer),
            _layer_spec((1, W), layer),
        ],
        out_specs=pl.BlockSpec((None, NT, W), lambda b: (b, 0, 0)),
        compiler_params=_params("arbitrary"),
        name="pool",
    )(pool_u, w_bd, scale)


def _ssm_kernel(u_ref, bmat_ref, cmat_ref, are_ref, aim_ref, y_ref, upad_ref, state_ref,
                *bufs, n_chunks):
    g = pl.program_id(0)
    nb = u_ref.shape[0]
    ts, pitch = SCAN_STEPS, SCAN_PITCH
    n_seq = 2 * n_chunks
    nk = SSM_LANES // LANES
    bu_even, bu_odd, s_even, s_odd = bufs[0:2], bufs[2:4], bufs[4:6], bufs[6:8]

    @pl.when(g == 0)
    def _():
        for r in (upad_ref, state_ref) + tuple(bufs):
            r[...] = jnp.zeros_like(r)

    def direction(i):
        return jnp.clip(i, 0, n_seq - 1) // n_chunks

    def half_rows(i, h):
        off = jnp.where(direction(i) == 0, h, 1 - h) * ts
        return pl.ds(pl.multiple_of(off, ts), ts)

    def stage_b(i, h, bu_ref):
        rows = half_rows(i, h)
        for b in range(nb):
            upad_ref[b * pitch:b * pitch + ts, :] = u_ref[b, rows, :]
        res = _dot(upad_ref[...].astype(bf16), bmat_ref[direction(i)])
        for k in range(2 * nk):
            bu_ref[k] = res[:, k * LANES:(k + 1) * LANES]

    def stage_scan(i, bu_ref, s_ref):
        ic = jnp.clip(i, 0, n_seq - 1)
        d = ic // n_chunks
        keep = jnp.where((ic % n_chunks) == 0, 0.0, 1.0)
        a_re = are_ref[d]
        a_im = aim_ref[d]
        st_re = [state_ref[:, k * LANES:(k + 1) * LANES] * keep for k in range(nk)]
        st_im = [state_ref[:, (nk + k) * LANES:(nk + k + 1) * LANES] * keep for k in range(nk)]
        for t in range(ts):
            tt = jnp.where(d == 0, t, ts - 1 - t)
            rows = pl.ds(tt, nb, stride=pitch)
            for k in range(nk):
                ar = a_re[:, k * LANES:(k + 1) * LANES]
                ai = a_im[:, k * LANES:(k + 1) * LANES]
                n_re = ar * st_re[k] - ai * st_im[k] + bu_ref[k, rows, :]
                n_im = ar * st_im[k] + ai * st_re[k] + bu_ref[nk + k, rows, :]
                s_ref[k, rows, :] = n_re
                s_ref[nk + k, rows, :] = n_im
                st_re[k], st_im[k] = n_re, n_im
        for k in range(nk):
            state_ref[:, k * LANES:(k + 1) * LANES] = st_re[k]
            state_ref[:, (nk + k) * LANES:(nk + k + 1) * LANES] = st_im[k]

    def stage_c(i, h, s_ref):
        s = jnp.concatenate([s_ref[k] for k in range(2 * nk)], axis=1).astype(bf16)
        yv = _dot(s, cmat_ref[direction(i)])
        rows = half_rows(i, h)
        for b in range(nb):
            y_ref[b, rows, :] = yv[b * pitch:b * pitch + ts, :]

    def step(bu_write, bu_read, s_write, s_read):
        for h in range(2):
            stage_scan(2 * g - 2 + h, bu_read[h], s_write[h])
            stage_b(2 * g + h, h, bu_write[h])
            stage_c(2 * g - 4 + h, h, s_read[h])

    @pl.when(g % 2 == 0)
    def _():
        step(bu_even, bu_odd, s_even, s_odd)

    @pl.when(g % 2 == 1)
    def _():
        step(bu_odd, bu_even, s_odd, s_even)


def _ssm(layer, u, bmat, cmat, a_re, a_im, n_lat):
    B, NT, W = u.shape
    ts = SCAN_STEPS
    n_chunks = NT // ts
    n_blocks = n_chunks // 2
    nx = n_lat // (2 * ts)
    nc = n_blocks - nx

    def block(step):
        step = jnp.clip(step, 0, 2 * n_blocks - 1)
        d = step // n_blocks
        j = step % n_blocks
        fwd = jnp.where(j < nc, nx + j, j - nc)
        back = jnp.where(j < nc, n_blocks - 1 - j, nx - 1 - (j - nc))
        return d, jnp.where(d == 0, fwd, back)

    rows = B * SCAN_PITCH
    slab = pltpu.VMEM((2 * SSM_LANES // LANES, rows, LANES), f32)
    return pl.pallas_call(
        functools.partial(_ssm_kernel, n_chunks=n_chunks),
        out_shape=jax.ShapeDtypeStruct((2, B, NT, W), f32),
        grid=(2 * n_blocks + 2,),
        in_specs=[
            pl.BlockSpec((B, 2 * ts, W), lambda g: (0, block(g)[1], 0)),
            _layer_spec((2, W, 2 * SSM_LANES), layer),
            _layer_spec((2, 2 * SSM_LANES, W), layer),
            _layer_spec((2, B, SSM_LANES), layer),
            _layer_spec((2, B, SSM_LANES), layer),
        ],
        out_specs=pl.BlockSpec((None, B, 2 * ts, W),
                               lambda g: (block(g - 2)[0], 0, block(g - 2)[1], 0)),
        scratch_shapes=[pltpu.VMEM((rows, W), f32), pltpu.VMEM((B, 2 * SSM_LANES), f32)]
        + [slab] * 8,
        compiler_params=_params("arbitrary"),
        name="ssm_scan",
    )(u, bmat, cmat, a_re, a_im)


def _attend_group(q_pairs, keys, vals, masks, sinks):
    nq = q_pairs[0].shape[0]
    lane = lax.broadcasted_iota(jnp.int32, (nq, LANES), 1)
    low = lane < HEAD_DIM
    zero = jnp.zeros((nq, LANES), bf16)
    lhs = jnp.concatenate([jnp.where(keep, qp, zero) for qp in q_pairs for keep in (low, ~low)],
                          axis=0)
    scores = [_dot_nt(lhs, kk) for kk in keys]
    scores = [s if mk is None else s + mk for s, mk in zip(scores, masks)]
    row = lax.broadcasted_iota(jnp.int32, (4 * nq, 1), 0)
    sink = jnp.where(row < 2 * nq, jnp.where(row < nq, sinks[0], sinks[1]),
                     jnp.where(row < 3 * nq, sinks[2], sinks[3])) * LOG2_E
    cols = [s[:, c:c + LANES] for s in scores for c in range(0, s.shape[1], LANES)]
    m = jnp.maximum(sink, jnp.max(functools.reduce(jnp.maximum, cols), axis=-1, keepdims=True))
    out = jnp.zeros((4 * nq, LANES), f32)
    for s, vv in zip(scores, vals):
        out = out + _dot(jnp.exp2(s - m).astype(bf16), vv)
    out = out / (out[:, HEAD_DIM:HEAD_DIM + 1] + jnp.exp2(sink - m))
    blocks = [out[h * nq:(h + 1) * nq] for h in range(4)]
    return [jnp.where(low, blocks[2 * p], pltpu.roll(blocks[2 * p + 1], HEAD_DIM, 1))
            for p in range(2)]


def _attn_kernel(sink_ref, q_ref, k2_ref, v2_ref, o_ref, *, n_lat):
    i = pl.program_id(1)
    nq = Q_TILE
    n_sub = q_ref.shape[0] // nq
    n_ctx = k2_ref.shape[0] - n_lat
    n_lat_steps = n_lat // (n_sub * nq)
    span = 3 * WINDOW

    def run(rq, keys_rows, masks):
        for hk in range(N_KV_HEADS):
            keys = [k2_ref[r, hk * LANES:(hk + 1) * LANES] for r in keys_rows]
            vals = [v2_ref[r, hk * LANES:(hk + 1) * LANES] for r in keys_rows]
            c0 = hk * GROUP * HEAD_DIM
            outs = _attend_group([q_ref[rq, c0:c0 + LANES], q_ref[rq, c0 + LANES:c0 + 2 * LANES]],
                                 keys, vals, masks, [sink_ref[hk * GROUP + h] for h in range(GROUP)])
            for p, o in enumerate(outs):
                o_ref[rq, c0 + p * LANES:c0 + (p + 1) * LANES] = o.astype(bf16)

    ctx_rows = pl.ds(n_lat, n_ctx)

    @pl.when(i >= n_lat_steps)
    def _():
        for sub in range(n_ctx // nq):
            run(slice(sub * nq, (sub + 1) * nq), [ctx_rows], [None])

    @pl.when(i < n_lat_steps)
    def _():
        for sub in range(n_sub):
            q0 = (i * n_sub + sub) * nq
            start = jnp.clip(q0 - WINDOW, 0, n_lat - span)
            q_pos = q0 + lax.broadcasted_iota(jnp.int32, (GROUP * nq, span), 0) % nq
            k_pos = start + lax.broadcasted_iota(jnp.int32, (GROUP * nq, span), 1)
            bias = jnp.where(jnp.abs(k_pos - q_pos) <= WINDOW, 0.0, -jnp.inf)
            win_rows = pl.ds(pl.multiple_of(start, WINDOW), span)
            run(slice(sub * nq, (sub + 1) * nq), [win_rows, ctx_rows], [bias, None])


def _attention(layer, sink, q, k2, v2, n_lat):
    B, NT, _ = q.shape
    return pl.pallas_call(
        functools.partial(_attn_kernel, n_lat=n_lat),
        out_shape=jax.ShapeDtypeStruct((B, NT, ATTN_WIDTH), bf16),
        grid=(B, pl.cdiv(NT, Q_BLOCK)),
        in_specs=[
            pl.BlockSpec(memory_space=pltpu.SMEM),
            pl.BlockSpec((None, Q_BLOCK, ATTN_WIDTH), lambda b, i: (b, i, 0)),
            pl.BlockSpec((None, NT, 2 * KV_WIDTH), lambda b, i: (b, 0, 0)),
            pl.BlockSpec((None, NT, 2 * KV_WIDTH), lambda b, i: (b, 0, 0)),
        ],
        out_specs=pl.BlockSpec((None, Q_BLOCK, ATTN_WIDTH), lambda b, i: (b, i, 0)),
        compiler_params=_params("arbitrary", "arbitrary"),
        name="attention",
    )(sink[layer], q, k2, v2)


def _outffn_kernel(*refs, with_ctx):
    if not with_ctx:
        _outffn_rows(refs[0], *refs[1:-1], refs[-1])
        return
    x_ref, c_ref, shared, xo_ref, co_ref = refs[0], refs[1], refs[2:-2], refs[-2], refs[-1]
    is_ctx = pl.program_id(1) == pl.num_programs(1) - 1

    @pl.when(jnp.logical_not(is_ctx))
    def _():
        _outffn_rows(x_ref, *shared, xo_ref)

    @pl.when(is_ctx)
    def _():
        _outffn_rows(c_ref, *shared, co_ref)


def _outffn_rows(x_ref, mod_ref, po_ref, u_ref, y0_ref, y1_ref, ao_ref, d_ref, gw_ref, gb_ref,
                 wo_ref, g_ref, wg_ref, wu_ref, wd_ref, o_ref):
    o1 = POOL_WIDTH
    o2 = POOL_WIDTH + SSM_WIDTH
    d_ff = wg_ref.shape[1]

    def mixer(r):
        y = d_ref[...] * u_ref[r, :] + y0_ref[r, :] + y1_ref[r, :]
        ge = _gelu_tanh(y)
        so = ge * _sigmoid(_dot(ge.astype(bf16), gw_ref[...]) + gb_ref[...])
        mix = (_dot(po_ref[r, :], wo_ref[0:o1, :]) + _dot(so.astype(bf16), wo_ref[o1:o2, :])
               + _dot(ao_ref[r, :], wo_ref[o2:, :]))
        x1 = x_ref[r, :] + mod_ref[2:3, :] * mix
        return x1, _rmsnorm_mod(x1, g_ref[...], mod_ref[4:5, :], mod_ref[3:4, :]).astype(bf16)

    def ffn(r, x1, h):
        acc = jnp.zeros_like(x1)
        for c0 in range(0, d_ff, FF_CHUNK):
            c1 = min(c0 + FF_CHUNK, d_ff)
            gate = _dot(h, wg_ref[:, c0:c1])
            up = _dot(h, wu_ref[:, c0:c1])
            acc = acc + _dot((_silu(gate) * up).astype(bf16), wd_ref[c0:c1, :])
        o_ref[r, :] = x1 + mod_ref[5:6, :] * acc

    _chained(x_ref.shape[0], mixer, ffn)


def _outffn(layer, x, ctx, mod, po, u, y, ao, d_skip, glu_w, glu_b, w_out, g, w_gate, w_up, w_down):
    B, L, D = x.shape
    blk = LATENT_BLOCK
    nj = L // blk
    with_ctx = ctx is not None
    d_ff = w_gate.shape[2]
    tok = lambda w: pl.BlockSpec((None, blk, w), lambda b, j: (b, j, 0))
    ydir = lambda d: pl.BlockSpec((None, None, blk, SSM_WIDTH), lambda b, j: (d, b, j, 0))
    x_spec = pl.BlockSpec((None, blk, D), lambda b, j: (b, jnp.minimum(j, nj - 1), 0))
    c_specs, c_shapes, c_args = [], [], []
    if with_ctx:
        c_specs = [pl.BlockSpec((None, ctx.shape[1], D), lambda b, j: (b, 0, 0))]
        c_shapes = [jax.ShapeDtypeStruct(ctx.shape, f32)]
        c_args = [ctx]
    outs = pl.pallas_call(
        functools.partial(_outffn_kernel, with_ctx=with_ctx),
        out_shape=[jax.ShapeDtypeStruct((B, L, D), f32)] + c_shapes,
        grid=(B, nj + len(c_args)),
        in_specs=[x_spec] + c_specs + [
            pl.BlockSpec((None, None, N_MOD, D), lambda b, j: (layer, _mod_row(j, nj, b, B), 0, 0)),
            tok(POOL_WIDTH),
            tok(SSM_WIDTH),
            ydir(0),
            ydir(1),
            tok(ATTN_WIDTH),
            _layer_spec((1, SSM_WIDTH), layer),
            _layer_spec((SSM_WIDTH, SSM_WIDTH), layer),
            _layer_spec((1, SSM_WIDTH), layer),
            _layer_spec((D, D), layer),
            _layer_spec((1, D), layer),
            _layer_spec((D, d_ff), layer),
            _layer_spec((D, d_ff), layer),
            _layer_spec((d_ff, D), layer),
        ],
        out_specs=[x_spec] + c_specs,
        compiler_params=_params("arbitrary", "arbitrary"),
        name="outproj_ffn",
    )(x, *c_args, mod, po, u, y, y, ao, d_skip, glu_w, glu_b, w_out, g, w_gate, w_up, w_down)
    return (outs[0], outs[1]) if with_ctx else (outs[0], None)


def _rope_tables(n_lat, n_ctx):
    pos = jnp.arange(n_lat)
    row = (pos // GRID_W).astype(f32)
    col = (pos % GRID_W).astype(f32)
    inv = jnp.power(ROPE_BASE, -jnp.arange(ROPE_FREQS, dtype=f32) / ROPE_FREQS)
    lane = jnp.arange(LANES)
    dim = lane % HEAD_DIM
    freq = inv[dim % ROPE_FREQS]
    ang = jnp.where((dim // (2 * ROPE_FREQS)) == 0, row[:, None], col[:, None]) * freq[None, :]
    sign = jnp.where((dim % (2 * ROPE_FREQS)) < ROPE_FREQS, -1.0, 1.0).astype(f32)
    cos = jnp.concatenate([jnp.cos(ang), jnp.ones((n_ctx, LANES), f32)], axis=0)
    sin = jnp.concatenate([jnp.sin(ang) * sign, jnp.zeros((n_ctx, LANES), f32)], axis=0)
    return cos, sin


def _block_diag(blocks):
    *lead, g, r, c = blocks.shape
    tiled = jnp.tile(blocks.reshape(*lead, g * r, c), (1,) * len(lead) + (1, g))
    on_diag = (jnp.arange(g * r)[:, None] // r) == (jnp.arange(g * c)[None, :] // c)
    return jnp.where(on_diag, tiled, jnp.zeros((), blocks.dtype))


def _ssm_matrices(a_re, a_im, log_dt, b_re, b_im, c_re, c_im, n_batch):
    lr = jnp.minimum(a_re.astype(f32), -1e-4)
    li = a_im.astype(f32)
    dt = jnp.exp(log_dt.astype(f32))[..., None]
    mag = jnp.exp(lr * dt)
    ar = mag * jnp.cos(li * dt)
    ai = mag * jnp.sin(li * dt)
    den = lr * lr + li * li
    qr = ((ar - 1.0) * lr + ai * li) / den
    qi = (ai * lr - (ar - 1.0) * li) / den
    bbr = qr[..., None] * b_re - qi[..., None] * b_im
    bbi = qr[..., None] * b_im + qi[..., None] * b_re
    bd = lambda a: _block_diag(jnp.swapaxes(a, -1, -2).astype(bf16))
    bmat = jnp.concatenate([bd(bbr), bd(bbi)], axis=-1)
    cmat = jnp.concatenate([bd(c_re), bd(-c_im)], axis=-2)
    shape = a_re.shape[:2] + (n_batch, SSM_LANES)
    a_re_b = jnp.broadcast_to(ar.reshape(a_re.shape[:2] + (1, SSM_LANES)), shape)
    a_im_b = jnp.broadcast_to(ai.reshape(a_re.shape[:2] + (1, SSM_LANES)), shape)
    return bmat.astype(bf16), cmat.astype(bf16), a_re_b, a_im_b


def kernel(x, c, ctx, c_ctx, w_mod, b_mod, norm_mix, norm_ffn, w_in, w_out, pool_w, pool_scale, ssm_a_re, ssm_a_im, ssm_log_dt, ssm_b_re, ssm_b_im, ssm_c_re, ssm_c_im, ssm_d, ssm_glu_w, ssm_glu_b, q_norm, k_norm, attn_sink, ffn_w_gate, ffn_w_up, ffn_w_down):
    B, L, D = x.shape
    Lc = ctx.shape[1]
    NT = L + Lc
    depth = w_mod.shape[0]
    assert Lc <= CHAIN_ROWS and L % LATENT_BLOCK == 0 and L % GRID_W == 0 and L >= 3 * WINDOW
    assert B == SUBLANES and w_in.shape[2] == POOL_WIDTH + SSM_WIDTH + ATTN_WIDTH + 2 * KV_WIDTH
    assert w_mod.shape[2] == N_MOD * D and Lc % (2 * SCAN_STEPS) == 0 and L % (2 * SCAN_STEPS) == 0

    c16 = jnp.concatenate([c, c_ctx[None, :], jnp.zeros((16 - B - 1, D), f32)], axis=0)
    mod = _modulation(c16, w_mod, b_mod).reshape(depth, 16, N_MOD, D)

    cos_t, sin_t = _rope_tables(L, LATENT_BLOCK)
    head = jnp.arange(ATTN_WIDTH) // HEAD_DIM
    ones = jnp.where(head[:, None] == head[None, :], 1.0 / HEAD_DIM, 0.0).astype(bf16)
    qn = (jnp.tile(q_norm, (1, N_HEADS)) * (HEAD_DIM ** -0.5 * LOG2_E)).reshape(depth, 1, ATTN_WIDTH)
    kn = jnp.tile(k_norm, (1, N_KV_HEADS)).reshape(depth, 1, KV_WIDTH)
    bmat, cmat, a_re, a_im = _ssm_matrices(ssm_a_re, ssm_a_im, ssm_log_dt, ssm_b_re, ssm_b_im,
                                           ssm_c_re, ssm_c_im, B)
    pool_bd = _block_diag(pool_w).astype(bf16)
    w_in_b, w_out_b, glu_w_b = w_in.astype(bf16), w_out.astype(bf16), ssm_glu_w.astype(bf16)
    w_gate_b, w_up_b, w_down_b = ffn_w_gate.astype(bf16), ffn_w_up.astype(bf16), ffn_w_down.astype(bf16)
    row = lambda a: a.reshape(depth, 1, a.shape[-1])

    for l in range(depth):
        pool_u, u, q, k2, v2 = _inproj(l, x, ctx, mod, row(norm_mix), w_in_b, cos_t, sin_t, qn, kn,
                                       ones)
        po = _pool(l, pool_u, pool_bd, row(pool_scale), L)
        y = _ssm(l, u, bmat, cmat, a_re, a_im, L)
        ao = _attention(l, attn_sink, q, k2, v2, L)
        x, ctx = _outffn(l, x, ctx if l < depth - 1 else None, mod, po, u, y, ao, row(ssm_d),
                         glu_w_b, row(ssm_glu_b), w_out_b, row(norm_ffn), w_gate_b, w_up_b, w_down_b)
    return x
```
